```python
import jax, jax.numpy as jnp
from jax import lax
import numpy as np

D_MODEL = 1024
BATCH = 8
SEQ = 8192
DEPTH = 2

GRID_W = 64
ROPE_THETA = 10000.0
Q_BLOCK = 128
EPS = 1e-6
N_MIXERS = 2

N_MEM = 256
MEM_HEADS = 4
MEM_HEAD_DIM = 128
MEM_W = MEM_HEADS * MEM_HEAD_DIM

MLA_HEADS = 8
MLA_Q_RANK = 384
MLA_KV_RANK = 256
MLA_NOPE = 128
MLA_ROPE = 64
MLA_V = 128
MLA_IN_W = MLA_Q_RANK + MLA_KV_RANK + MLA_ROPE

GQA_HEADS = 8
GQA_KV_HEADS = 2
GQA_HEAD_DIM = 128
GQA_Q_W = GQA_HEADS * GQA_HEAD_DIM
GQA_KV_W = GQA_KV_HEADS * GQA_HEAD_DIM
GQA_IN_W = GQA_Q_W + 2 * GQA_KV_W

MIX_W = MLA_HEADS * MLA_V + MEM_W

D_FF = -(-8 * D_MODEL // (3 * 256)) * 256

kernel_name = "hybrid_mla_gqa_memory_encoder"


def _rmsnorm(x, g):
    x32 = x.astype(jnp.float32)
    y = x32 * lax.rsqrt(jnp.mean(x32 * x32, axis=-1, keepdims=True) + EPS)
    return (y * g.astype(jnp.float32)).astype(x.dtype)


def _axial_rope_tables(seq, dim, dtype):
    rows = seq // GRID_W
    row = jnp.repeat(jnp.arange(rows, dtype=jnp.float32), GRID_W)
    col = jnp.tile(jnp.arange(GRID_W, dtype=jnp.float32), rows)
    axis_dim = dim // 2
    inv = ROPE_THETA ** (-jnp.arange(0, axis_dim, 2, dtype=jnp.float32) / axis_dim)
    ang_r = row[:, None] * inv
    ang_c = col[:, None] * inv
    tabs = (jnp.cos(ang_r), jnp.sin(ang_r), jnp.cos(ang_c), jnp.sin(ang_c))
    return tuple(t[:, None, :].astype(dtype) for t in tabs)


def _rope_1d(x, cos, sin):
    x1, x2 = jnp.split(x, 2, axis=-1)
    return jnp.concatenate([x1 * cos - x2 * sin, x2 * cos + x1 * sin], axis=-1)


def _apply_axial_rope(x, tabs):
    cos_r, sin_r, cos_c, sin_c = tabs
    xr, xc = jnp.split(x, 2, axis=-1)
    return jnp.concatenate([_rope_1d(xr, cos_r, sin_r), _rope_1d(xc, cos_c, sin_c)], axis=-1)


def _blocked_attention(q_parts, k_parts, v, scale):
    B, S, Hk, Dv = v.shape
    H = q_parts[0].shape[2]
    G = H // Hk
    nb = S // Q_BLOCK
    qb = tuple(q.reshape(B, nb, Q_BLOCK, Hk, G, q.shape[-1]).transpose(1, 0, 2, 3, 4, 5)
               for q in q_parts)

    def one_block(qs):
        s = None
        for qp, kp in zip(qs, k_parts):
            eq = 'bqkgd,bskd->bkgqs' if kp.ndim == 4 else 'bqkgd,bsd->bkgqs'
            t = jnp.einsum(eq, qp, kp, preferred_element_type=jnp.float32)
            s = t if s is None else s + t
        p = jax.nn.softmax(s * scale, axis=-1).astype(v.dtype)
        return jnp.einsum('bkgqs,bskd->bqkgd', p, v)

    out = lax.map(one_block, qb)
    return out.transpose(1, 0, 2, 3, 4, 5).reshape(B, S, H, Dv)


def _mla_mixer(p, q_a_norm, w_q_b, kv_a_norm, w_kv_b, q_norm, k_norm, rope):
    B, S, _ = p.shape
    c_q = p[..., :MLA_Q_RANK]
    c_kv = p[..., MLA_Q_RANK:MLA_Q_RANK + MLA_KV_RANK]
    k_pe = p[..., MLA_Q_RANK + MLA_KV_RANK:]
    q = (_rmsnorm(c_q, q_a_norm) @ w_q_b).reshape(B, S, MLA_HEADS, MLA_NOPE + MLA_ROPE)
    kv = (_rmsnorm(c_kv, kv_a_norm) @ w_kv_b).reshape(B, S, MLA_HEADS, MLA_NOPE + MLA_V)
    q_nope = _rmsnorm(q[..., :MLA_NOPE], q_norm[:MLA_NOPE])
    q_pe = _apply_axial_rope(_rmsnorm(q[..., MLA_NOPE:], q_norm[MLA_NOPE:]), rope)
    k_nope = _rmsnorm(kv[..., :MLA_NOPE], k_norm[:MLA_NOPE])
    v = kv[..., MLA_NOPE:]
    k_pe = _apply_axial_rope(_rmsnorm(k_pe, k_norm[MLA_NOPE:])[:, :, None, :], rope)[:, :, 0, :]
    out = _blocked_attention([q_nope, q_pe], [k_nope, k_pe], v,
                             (MLA_NOPE + MLA_ROPE) ** -0.5)
    return out.reshape(B, S, MLA_HEADS * MLA_V)


def _gqa_mixer(p, q_norm, k_norm, rope):
    B, S, _ = p.shape
    q = p[..., :GQA_Q_W].reshape(B, S, GQA_HEADS, GQA_HEAD_DIM)
    k = p[..., GQA_Q_W:GQA_Q_W + GQA_KV_W].reshape(B, S, GQA_KV_HEADS, GQA_HEAD_DIM)
    v = p[..., GQA_Q_W + GQA_KV_W:].reshape(B, S, GQA_KV_HEADS, GQA_HEAD_DIM)
    q = _apply_axial_rope(_rmsnorm(q, q_norm), rope)
    k = _apply_axial_rope(_rmsnorm(k, k_norm), rope)
    out = _blocked_attention([q], [k], v, GQA_HEAD_DIM ** -0.5)
    return out.reshape(B, S, GQA_Q_W)


def _memory_attention(q, mem_n, w_kv, q_norm, k_norm):
    B, S, _ = q.shape
    M = mem_n.shape[1]
    q = _rmsnorm(q.reshape(B, S, MEM_HEADS, MEM_HEAD_DIM), q_norm)
    kv = (mem_n @ w_kv).reshape(B, M, 2, MEM_HEADS, MEM_HEAD_DIM)
    k = _rmsnorm(kv[:, :, 0], k_norm)
    v = kv[:, :, 1]
    s = jnp.einsum('bqhd,bmhd->bhqm', q, k, preferred_element_type=jnp.float32)
    p = jax.nn.softmax(s * (MEM_HEAD_DIM ** -0.5), axis=-1).astype(v.dtype)
    return jnp.einsum('bhqm,bmhd->bqhd', p, v).reshape(B, S, MEM_W)


def _swiglu(h, w_gate_up, w_down):
    gu = h @ w_gate_up
    g, u = gu[..., :D_FF], gu[..., D_FF:]
    return (jax.nn.silu(g) * u) @ w_down


def setup_inputs(seed: int = 0) -> dict:
    key = jax.random.key(seed)
    ks = iter(jax.random.split(key, 32))
    n_mla = (DEPTH + 1) // 2
    n_gqa = DEPTH // 2

    def w(shape, fan_in):
        return jax.random.normal(next(ks), shape, jnp.float32) * (fan_in ** -0.5)

    def gain(shape):
        return 1.0 + 0.05 * jax.random.normal(next(ks), shape, jnp.float32)

    return {
        "x": jax.random.normal(next(ks), (BATCH, SEQ, D_MODEL), jnp.float32),
        "mem": jax.random.normal(next(ks), (BATCH, N_MEM, D_MODEL), jnp.float32),
        "mem_norm": gain((D_MODEL,)),
        "norm_mix": gain((DEPTH, D_MODEL)),
        "norm_ffn": gain((DEPTH, D_MODEL)),
        "w_out": w((DEPTH, MIX_W, D_MODEL), MIX_W),
        "w_mem_kv": w((DEPTH, D_MODEL, 2 * MEM_W), D_MODEL),
        "memq_norm": gain((DEPTH, MEM_HEAD_DIM)),
        "memk_norm": gain((DEPTH, MEM_HEAD_DIM)),
        "w_gate_up": w((DEPTH, D_MODEL, 2 * D_FF), D_MODEL),
        "w_down": w((DEPTH, D_FF, D_MODEL), D_FF),
        "mla_w_in": w((n_mla, D_MODEL, MLA_IN_W + MEM_W), D_MODEL),
        "mla_q_a_norm": gain((n_mla, MLA_Q_RANK)),
        "mla_w_q_b": w((n_mla, MLA_Q_RANK, MLA_HEADS * (MLA_NOPE + MLA_ROPE)), MLA_Q_RANK),
        "mla_kv_a_norm": gain((n_mla, MLA_KV_RANK)),
        "mla_w_kv_b": w((n_mla, MLA_KV_RANK, MLA_HEADS * (MLA_NOPE + MLA_V)), MLA_KV_RANK),
        "mla_q_norm": gain((n_mla, MLA_NOPE + MLA_ROPE)),
        "mla_k_norm": gain((n_mla, MLA_NOPE + MLA_ROPE)),
        "gqa_w_in": w((n_gqa, D_MODEL, GQA_IN_W + MEM_W), D_MODEL),
        "gqa_q_norm": gain((n_gqa, GQA_HEAD_DIM)),
        "gqa_k_norm": gain((n_gqa, GQA_HEAD_DIM)),
    }


def reference(x, mem, mem_norm, norm_mix, norm_ffn, w_out, w_mem_kv, memq_norm, memk_norm,
              w_gate_up, w_down, mla_w_in, mla_q_a_norm, mla_w_q_b, mla_kv_a_norm, mla_w_kv_b,
              mla_q_norm, mla_k_norm, gqa_w_in, gqa_q_norm, gqa_k_norm):
    S = x.shape[1]
    rope_mla = _axial_rope_tables(S, MLA_ROPE, x.dtype)
    rope_gqa = _axial_rope_tables(S, GQA_HEAD_DIM, x.dtype)
    mem_n = _rmsnorm(mem, mem_norm)
    for i in range(DEPTH):
        j = i // N_MIXERS
        h = _rmsnorm(x, norm_mix[i])
        if i % N_MIXERS == 0:
            proj = h @ mla_w_in[j]
            mix = _mla_mixer(proj[..., :MLA_IN_W], mla_q_a_norm[j], mla_w_q_b[j],
                             mla_kv_a_norm[j], mla_w_kv_b[j], mla_q_norm[j], mla_k_norm[j],
                             rope_mla)
            q_mem = proj[..., MLA_IN_W:]
        else:
            proj = h @ gqa_w_in[j]
            mix = _gqa_mixer(proj[..., :GQA_IN_W], gqa_q_norm[j], gqa_k_norm[j], rope_gqa)
            q_mem = proj[..., GQA_IN_W:]
        mem_out = _memory_attention(q_mem, mem_n, w_mem_kv[i], memq_norm[i], memk_norm[i])
        x = x + jnp.concatenate([mix, mem_out], axis=-1) @ w_out[i]
        x = x + _swiglu(_rmsnorm(x, norm_ffn[i]), w_gate_up[i], w_down[i])
    return x
```

```python
import functools

import jax
import jax.numpy as jnp
from jax import lax
from jax.experimental import pallas as pl
from jax.experimental.pallas import tpu as pltpu

D_MODEL = 1024
GRID_W = 64
ROPE_THETA = 10000.0
EPS = 1e-6

MEM_HEADS = 4
HEAD = 128
MEM_W = MEM_HEADS * HEAD

MLA_HEADS = 8
MLA_Q_RANK = 384
MLA_KV_RANK = 256
MLA_NOPE = 128
MLA_ROPE = 64
MLA_QK = 2 * HEAD

GQA_HEADS = 8
GQA_KV_HEADS = 2

D_FF = 2816
FF_CHUNK = 256

VMEM_LIMIT = 56 * 1024 * 1024

BF16 = jnp.bfloat16
F32 = jnp.float32


def _params(n_axes):
    return pltpu.CompilerParams(dimension_semantics=("arbitrary",) * n_axes,
                                vmem_limit_bytes=VMEM_LIMIT)


def _rms(x, width=None):
    n = x.shape[-1] if width is None else width
    ms = jnp.sum(x * x, axis=-1, keepdims=True) * (1.0 / n)
    return x * lax.rsqrt(ms + EPS)


def _rope(x, cos, sin, half):
    lane = lax.broadcasted_iota(jnp.int32, x.shape, 1)
    upper = pltpu.roll(x, HEAD - half, axis=1)
    lower = pltpu.roll(x, half, axis=1)
    swapped = jnp.where((lane & (2 * half - 1)) < half, upper, lower)
    return x * cos + swapped * sin


def _nt_dot(a, b):
    return lax.dot_general(a, b, (((1,), (1,)), ((), ())), preferred_element_type=F32)


def _memkv_kernel(mem_ref, g_ref, w_ref, kn_ref, k_ref, v_ref):
    mem_n = (_rms(mem_ref[...]) * g_ref[...]).astype(BF16)
    kv = jnp.dot(mem_n, w_ref[...], preferred_element_type=F32)
    for h in range(MEM_HEADS):
        kh = _rms(kv[:, h * HEAD:(h + 1) * HEAD]) * kn_ref[...]
        k_ref[:, h * HEAD:(h + 1) * HEAD] = kh.astype(BF16)
    v_ref[...] = kv[:, MEM_W:].astype(BF16)


def _memkv(mem, mem_norm, w_mem_kv, memk_norm):
    depth = w_mem_kv.shape[0]
    batch, n_mem, _ = mem.shape
    out = jax.ShapeDtypeStruct((depth, batch, n_mem, MEM_W), BF16)
    return pl.pallas_call(
        _memkv_kernel,
        grid=(depth, batch),
        in_specs=[
            pl.BlockSpec((None, n_mem, D_MODEL), lambda l, b: (b, 0, 0)),
            pl.BlockSpec((1, D_MODEL), lambda l, b: (0, 0)),
            pl.BlockSpec((None, D_MODEL, 2 * MEM_W), lambda l, b: (l, 0, 0)),
            pl.BlockSpec((None, 1, HEAD), lambda l, b: (l, 0, 0)),
        ],
        out_specs=[
            pl.BlockSpec((None, None, n_mem, MEM_W), lambda l, b: (l, b, 0, 0)),
            pl.BlockSpec((None, None, n_mem, MEM_W), lambda l, b: (l, b, 0, 0)),
        ],
        out_shape=[out, out],
        compiler_params=_params(2),
        name="mem_kv",
    )(mem, mem_norm.reshape(1, D_MODEL), w_mem_kv.astype(BF16),
      memk_norm.reshape(depth, 1, HEAD))


def _proj_mla_kernel(x_ref, g_ref, w_in_ref, qa_ref, w_qb_ref, kva_ref, w_kvb_ref,
                     qn_ref, qp_ref, kn_ref, kp_ref, mq_ref, cos_ref, sin_ref,
                     q_ref, k_ref, v_ref, qm_ref, *, qk_scale, mem_scale):
    h = (_rms(x_ref[...]) * g_ref[...]).astype(BF16)
    proj = jnp.dot(h, w_in_ref[...], preferred_element_type=F32)
    o_kv = MLA_Q_RANK
    o_pe = o_kv + MLA_KV_RANK
    o_qm = o_pe + HEAD
    c_q = (_rms(proj[:, :o_kv]) * qa_ref[...]).astype(BF16)
    c_kv = (_rms(proj[:, o_kv:o_pe]) * kva_ref[...]).astype(BF16)
    q = jnp.dot(c_q, w_qb_ref[...], preferred_element_type=F32)
    kv = jnp.dot(c_kv, w_kvb_ref[...], preferred_element_type=F32)
    cos = cos_ref[...]
    sin = sin_ref[...]
    half = MLA_ROPE // 4
    k_pe = _rope(_rms(proj[:, o_pe:o_qm], MLA_ROPE) * kp_ref[...], cos, sin, half).astype(BF16)
    n_w = MLA_HEADS * HEAD
    for i in range(MLA_HEADS):
        lo = i * HEAD
        q_n = _rms(q[:, lo:lo + HEAD]) * qn_ref[...] * qk_scale
        q_p = _rope(_rms(q[:, n_w + lo:n_w + lo + HEAD], MLA_ROPE) * qp_ref[...], cos, sin, half)
        q_ref[:, 2 * lo:2 * lo + HEAD] = q_n.astype(BF16)
        q_ref[:, 2 * lo + HEAD:2 * lo + 2 * HEAD] = (q_p * qk_scale).astype(BF16)
        k_n = _rms(kv[:, lo:lo + HEAD]) * kn_ref[...]
        k_ref[:, 2 * lo:2 * lo + HEAD] = k_n.astype(BF16)
        k_ref[:, 2 * lo + HEAD:2 * lo + 2 * HEAD] = k_pe
    v_ref[...] = kv[:, n_w:].astype(BF16)
    for i in range(MEM_HEADS):
        lo = i * HEAD
        q_m = _rms(proj[:, o_qm + lo:o_qm + lo + HEAD]) * mq_ref[...] * mem_scale
        qm_ref[:, lo:lo + HEAD] = q_m.astype(BF16)


def _proj_gqa_kernel(x_ref, g_ref, w_in_ref, qn_ref, kn_ref, mq_ref, cos_ref, sin_ref,
                     q_ref, k_ref, v_ref, qm_ref, *, qk_scale, mem_scale):
    h = (_rms(x_ref[...]) * g_ref[...]).astype(BF16)
    proj = jnp.dot(h, w_in_ref[...], preferred_element_type=F32)
    cos = cos_ref[...]
    sin = sin_ref[...]
    half = HEAD // 4
    o_k = GQA_HEADS * HEAD
    o_v = o_k + GQA_KV_HEADS * HEAD
    o_qm = o_v + GQA_KV_HEADS * HEAD
    for i in range(GQA_HEADS):
        lo = i * HEAD
        q_h = _rope(_rms(proj[:, lo:lo + HEAD]) * qn_ref[...], cos, sin, half) * qk_scale
        q_ref[:, lo:lo + HEAD] = q_h.astype(BF16)
    for i in range(GQA_KV_HEADS):
        lo = i * HEAD
        k_h = _rope(_rms(proj[:, o_k + lo:o_k + lo + HEAD]) * kn_ref[...], cos, sin, half)
        k_ref[:, lo:lo + HEAD] = k_h.astype(BF16)
    v_ref[...] = proj[:, o_v:o_qm].astype(BF16)
    for i in range(MEM_HEADS):
        lo = i * HEAD
        q_m = _rms(proj[:, o_qm + lo:o_qm + lo + HEAD]) * mq_ref[...] * mem_scale
        qm_ref[:, lo:lo + HEAD] = q_m.astype(BF16)


def _row_tile(n_rows, seq, want):
    t = min(want, seq)
    assert seq % t == 0 and n_rows % t == 0
    return t


def _full(shape):
    return pl.BlockSpec(shape, lambda i: (0,) * len(shape))


def _proj_call(kernel, x2, seq, consts, tables, out_widths, tm):
    n_rows = x2.shape[0]
    n_pos = seq // tm
    row = lambda w: pl.BlockSpec((tm, w), lambda i: (i, 0))
    tab = pl.BlockSpec((tm, HEAD), lambda i: (i % n_pos, 0))
    return pl.pallas_call(
        kernel,
        grid=(n_rows // tm,),
        in_specs=[row(D_MODEL)] + [_full(c.shape) for c in consts] + [tab, tab],
        out_specs=[row(w) for w in out_widths],
        out_shape=[jax.ShapeDtypeStruct((n_rows, w), BF16) for w in out_widths],
        compiler_params=_params(1),
        name=kernel.func.__name__.strip("_"),
    )(x2, *consts, *tables)


def _attn_kernel(q_ref, k_ref, v_ref, o_ref, *, tk):
    q = q_ref[...]
    tq = q.shape[0]
    dv = v_ref.shape[-1]
    n_chunks = k_ref.shape[0] // tk

    def body(j, carry):
        m, l, acc = carry
        start = pl.multiple_of(j * tk, tk)
        s = _nt_dot(q, k_ref[pl.ds(start, tk), :])
        m_new = jnp.maximum(m, jnp.max(s, axis=-1, keepdims=True))
        alpha = jnp.exp(m - m_new)
        p = jnp.exp(s - m_new)
        l = alpha * l + jnp.sum(p, axis=-1, keepdims=True)
        acc = alpha * acc + jnp.dot(p.astype(BF16), v_ref[pl.ds(start, tk), :],
                                    preferred_element_type=F32)
        return m_new, l, acc

    init = (jnp.full((tq, 1), -jnp.inf, F32), jnp.zeros((tq, 1), F32), jnp.zeros((tq, dv), F32))
    _, l, acc = lax.fori_loop(0, n_chunks, body, init)
    o_ref[...] = (acc / l).astype(o_ref.dtype)


def _attention(q, k, v, n_heads, n_kv_heads, tq, tk):
    batch, seq, _ = q.shape
    dk = q.shape[-1] // n_heads
    dv = v.shape[-1] // n_kv_heads
    group = n_heads // n_kv_heads
    tq = min(tq, seq)
    tk = min(tk, seq)
    return pl.pallas_call(
        functools.partial(_attn_kernel, tk=tk),
        grid=(batch, n_heads, seq // tq),
        in_specs=[
            pl.BlockSpec((None, tq, dk), lambda b, h, i: (b, i, h)),
            pl.BlockSpec((None, seq, dk), lambda b, h, i: (b, 0, h // group)),
            pl.BlockSpec((None, seq, dv), lambda b, h, i: (b, 0, h // group)),
        ],
        out_specs=pl.BlockSpec((None, tq, dv), lambda b, h, i: (b, i, h)),
        out_shape=jax.ShapeDtypeStruct((batch, seq, n_heads * dv), BF16),
        compiler_params=_params(3),
        name="attention",
    )(q, k, v)


def _post_kernel(x_ref, mix_ref, qm_ref, km_ref, vm_ref, w_ref, o_ref):
    n_mix = mix_ref.shape[-1]
    y = x_ref[...] + jnp.dot(mix_ref[...], w_ref[:n_mix, :], preferred_element_type=F32)
    for i in range(MEM_HEADS):
        lo = i * HEAD
        s = _nt_dot(qm_ref[:, lo:lo + HEAD], km_ref[:, lo:lo + HEAD])
        e = jnp.exp(s - jnp.max(s, axis=-1, keepdims=True))
        pv = jnp.dot(e.astype(BF16), vm_ref[:, lo:lo + HEAD], preferred_element_type=F32)
        o_mem = (pv / jnp.sum(e, axis=-1, keepdims=True)).astype(BF16)
        y = y + jnp.dot(o_mem, w_ref[n_mix + lo:n_mix + lo + HEAD, :], preferred_element_type=F32)
    o_ref[...] = y


def _post(x2, mix2, qm2, k_mem, v_mem, w_out, seq, tm):
    n_rows = x2.shape[0]
    per_batch = seq // tm
    n_mem = k_mem.shape[1]
    row = lambda w: pl.BlockSpec((tm, w), lambda i: (i, 0))
    memspec = pl.BlockSpec((None, n_mem, MEM_W), lambda i: (i // per_batch, 0, 0))
    return pl.pallas_call(
        _post_kernel,
        grid=(n_rows // tm,),
        in_specs=[row(D_MODEL), row(mix2.shape[1]), row(MEM_W), memspec, memspec,
                  _full(w_out.shape)],
        out_specs=row(D_MODEL),
        out_shape=jax.ShapeDtypeStruct((n_rows, D_MODEL), F32),
        compiler_params=_params(1),
        name="mix_out",
    )(x2, mix2, qm2, k_mem, v_mem, w_out)


def _ffn_kernel(x_ref, g_ref, w_gu_ref, w_down_ref, o_ref):
    x = x_ref[...]
    h = (_rms(x) * g_ref[...]).astype(BF16)
    y = x
    for c in range(D_FF // FF_CHUNK):
        lo = c * FF_CHUNK
        gate = jnp.dot(h, w_gu_ref[:, lo:lo + FF_CHUNK], preferred_element_type=F32)
        up = jnp.dot(h, w_gu_ref[:, D_FF + lo:D_FF + lo + FF_CHUNK], preferred_element_type=F32)
        act = (gate * jax.nn.sigmoid(gate) * up).astype(BF16)
        y = y + jnp.dot(act, w_down_ref[lo:lo + FF_CHUNK, :], preferred_element_type=F32)
    o_ref[...] = y


def _ffn(x2, g, w_gu, w_down, tm):
    n_rows = x2.shape[0]
    row = pl.BlockSpec((tm, D_MODEL), lambda i: (i, 0))
    return pl.pallas_call(
        _ffn_kernel,
        grid=(n_rows // tm,),
        in_specs=[row, _full((1, D_MODEL)), _full(w_gu.shape), _full(w_down.shape)],
        out_specs=row,
        out_shape=jax.ShapeDtypeStruct((n_rows, D_MODEL), F32),
        compiler_params=_params(1),
        name="ffn",
    )(x2, g.reshape(1, D_MODEL), w_gu, w_down)


def _rope_tables(seq, dim):
    rows = seq // GRID_W
    row = jnp.repeat(jnp.arange(rows, dtype=F32), GRID_W)
    col = jnp.tile(jnp.arange(GRID_W, dtype=F32), rows)
    axis_dim = dim // 2
    inv = ROPE_THETA ** (-jnp.arange(0, axis_dim, 2, dtype=F32) / axis_dim)
    ang_r = row[:, None] * inv
    ang_c = col[:, None] * inv
    cos = jnp.concatenate([jnp.cos(ang_r)] * 2 + [jnp.cos(ang_c)] * 2, axis=-1)
    sin = jnp.concatenate([-jnp.sin(ang_r), jnp.sin(ang_r), -jnp.sin(ang_c), jnp.sin(ang_c)], axis=-1)
    pad = ((0, 0), (0, HEAD - dim))
    return jnp.pad(cos, pad), jnp.pad(sin, pad)


def _pad_lanes(v, width=HEAD):
    return jnp.pad(v, (0, width - v.shape[0])).reshape(1, width)


def kernel(x, mem, mem_norm, norm_mix, norm_ffn, w_out, w_mem_kv, memq_norm, memk_norm, w_gate_up, w_down, mla_w_in, mla_q_a_norm, mla_w_q_b, mla_kv_a_norm, mla_w_kv_b, mla_q_norm, mla_k_norm, gqa_w_in, gqa_q_norm, gqa_k_norm):
    batch, seq, _ = x.shape
    depth = norm_mix.shape[0]
    n_rows = batch * seq
    tm_proj = _row_tile(n_rows, seq, 512)
    tm_post = _row_tile(n_rows, seq, 512)
    tm_ffn = _row_tile(n_rows, seq, 512)
    mem_scale = HEAD ** -0.5

    k_mem, v_mem = _memkv(mem, mem_norm, w_mem_kv, memk_norm)
    x2 = x.reshape(n_rows, D_MODEL)
    for i in range(depth):
        j = i // 2
        g_mix = norm_mix[i].reshape(1, D_MODEL)
        mq = memq_norm[i].reshape(1, HEAD)
        if i % 2 == 0:
            w_in = mla_w_in[j]
            o_qm = MLA_Q_RANK + MLA_KV_RANK + MLA_ROPE
            w_in = jnp.concatenate(
                [w_in[:, :o_qm], jnp.zeros((D_MODEL, HEAD - MLA_ROPE), F32), w_in[:, o_qm:]], axis=1)
            w_qb = mla_w_q_b[j].reshape(MLA_Q_RANK, MLA_HEADS, MLA_NOPE + MLA_ROPE)
            w_qb = jnp.concatenate(
                [w_qb[:, :, :MLA_NOPE].reshape(MLA_Q_RANK, -1),
                 jnp.pad(w_qb[:, :, MLA_NOPE:], ((0, 0), (0, 0), (0, HEAD - MLA_ROPE))
                         ).reshape(MLA_Q_RANK, -1)], axis=1)
            w_kvb = mla_w_kv_b[j].reshape(MLA_KV_RANK, MLA_HEADS, 2 * HEAD)
            w_kvb = jnp.concatenate([w_kvb[:, :, :HEAD].reshape(MLA_KV_RANK, -1),
                                     w_kvb[:, :, HEAD:].reshape(MLA_KV_RANK, -1)], axis=1)
            consts = [g_mix, w_in.astype(BF16),
                      mla_q_a_norm[j].reshape(1, -1), w_qb.astype(BF16),
                      mla_kv_a_norm[j].reshape(1, -1), w_kvb.astype(BF16),
                      _pad_lanes(mla_q_norm[j][:MLA_NOPE]), _pad_lanes(mla_q_norm[j][MLA_NOPE:]),
                      _pad_lanes(mla_k_norm[j][:MLA_NOPE]), _pad_lanes(mla_k_norm[j][MLA_NOPE:]),
                      mq]
            kern = functools.partial(_proj_mla_kernel,
                                     qk_scale=(MLA_NOPE + MLA_ROPE) ** -0.5, mem_scale=mem_scale)
            q2, k2, v2, qm2 = _proj_call(
                kern, x2, seq, consts, _rope_tables(seq, MLA_ROPE),
                (MLA_HEADS * MLA_QK, MLA_HEADS * MLA_QK, MLA_HEADS * HEAD, MEM_W), tm_proj)
            n_heads, n_kv = MLA_HEADS, MLA_HEADS
        else:
            consts = [g_mix, gqa_w_in[j].astype(BF16),
                      gqa_q_norm[j].reshape(1, HEAD), gqa_k_norm[j].reshape(1, HEAD), mq]
            kern = functools.partial(_proj_gqa_kernel, qk_scale=HEAD ** -0.5, mem_scale=mem_scale)
            q2, k2, v2, qm2 = _proj_call(
                kern, x2, seq, consts, _rope_tables(seq, HEAD),
                (GQA_HEADS * HEAD, GQA_KV_HEADS * HEAD, GQA_KV_HEADS * HEAD, MEM_W), tm_proj)
            n_heads, n_kv = GQA_HEADS, GQA_KV_HEADS
        mix = _attention(q2.reshape(batch, seq, -1), k2.reshape(batch, seq, -1),
                         v2.reshape(batch, seq, -1), n_heads, n_kv, tq=512, tk=512)
        x2 = _post(x2, mix.reshape(n_rows, -1), qm2, k_mem[i], v_mem[i],
                   w_out[i].astype(BF16), seq, tm_post)
        x2 = _ffn(x2, norm_ffn[i], w_gate_up[i].astype(BF16), w_down[i].astype(BF16), tm_ffn)
    return x2.reshape(batch, seq, D_MODEL)
```

```python
import functools

import jax
import jax.numpy as jnp
from jax import lax
from jax.experimental import pallas as pl
from jax.experimental.pallas import tpu as pltpu

D_MODEL = 1024
GRID_W = 64
ROPE_THETA = 10000.0
EPS = 1e-6
LOG2E = 1.4426950408889634

MEM_HEADS = 4
HEAD = 128
MEM_W = MEM_HEADS * HEAD

MLA_HEADS = 8
MLA_Q_RANK = 384
MLA_KV_RANK = 256
MLA_NOPE = 128
MLA_ROPE = 64
MLA_QK = 2 * HEAD

GQA_HEADS = 8
GQA_KV_HEADS = 2

D_FF = 2816
FF_CHUNK = 256
COL_TILE = 256
EXP_ROWS = 64
SUM_ROWS = 16

VMEM_LIMIT = 56 * 1024 * 1024

BF16 = jnp.bfloat16
F32 = jnp.float32


def _params(n_axes):
    return pltpu.CompilerParams(dimension_semantics=("arbitrary",) * n_axes,
                                vmem_limit_bytes=VMEM_LIMIT)


def _rms(x, width=None):
    n = x.shape[-1] if width is None else width
    ms = jnp.sum(x * x, axis=-1, keepdims=True) * (1.0 / n)
    return x * lax.rsqrt(ms + EPS)


def _rope(x, cos, sin, half):
    lane = lax.broadcasted_iota(jnp.int32, x.shape, 1)
    upper = pltpu.roll(x, HEAD - half, axis=1)
    lower = pltpu.roll(x, half, axis=1)
    swapped = jnp.where((lane & (2 * half - 1)) < half, upper, lower)
    return x * cos + swapped * sin


def _nt_dot(a, b):
    return lax.dot_general(a, b, (((1,), (1,)), ((), ())), preferred_element_type=F32)


def _memkv_kernel(mem_ref, g_ref, w_ref, kn_ref, k_ref, v_ref):
    mem_n = (_rms(mem_ref[...]) * g_ref[...]).astype(BF16)
    kv = jnp.dot(mem_n, w_ref[...], preferred_element_type=F32)
    for h in range(MEM_HEADS):
        kh = _rms(kv[:, h * HEAD:(h + 1) * HEAD]) * kn_ref[...]
        k_ref[:, h * HEAD:(h + 1) * HEAD] = kh.astype(BF16)
    v_ref[...] = kv[:, MEM_W:].astype(BF16)


def _memkv(mem, mem_norm, w_mem_kv, memk_norm):
    depth = w_mem_kv.shape[0]
    batch, n_mem, _ = mem.shape
    out = jax.ShapeDtypeStruct((depth, batch, n_mem, MEM_W), BF16)
    return pl.pallas_call(
        _memkv_kernel,
        grid=(depth, batch),
        in_specs=[
            pl.BlockSpec((None, n_mem, D_MODEL), lambda l, b: (b, 0, 0)),
            pl.BlockSpec((1, D_MODEL), lambda l, b: (0, 0)),
            pl.BlockSpec((None, D_MODEL, 2 * MEM_W), lambda l, b: (l, 0, 0)),
            pl.BlockSpec((None, 1, HEAD), lambda l, b: (l, 0, 0)),
        ],
        out_specs=[
            pl.BlockSpec((None, None, n_mem, MEM_W), lambda l, b: (l, b, 0, 0)),
            pl.BlockSpec((None, None, n_mem, MEM_W), lambda l, b: (l, b, 0, 0)),
        ],
        out_shape=[out, out],
        compiler_params=_params(2),
        name="mem_kv",
    )(mem, mem_norm.reshape(1, D_MODEL), w_mem_kv.astype(BF16),
      memk_norm.reshape(depth, 1, HEAD))


def _store_vt(vt_ref, head, v):
    tm = v.shape[0]
    vt_ref[head, :HEAD, :] = v.T.astype(BF16)
    row = lax.broadcasted_iota(jnp.int32, (SUM_ROWS, tm), 0)
    vt_ref[head, HEAD:, :] = jnp.where(row == 0, 1.0, 0.0).astype(BF16)


def _proj_mla_kernel(x_ref, g_ref, w_in_ref, qa_ref, w_qb_ref, kva_ref, w_kvb_ref,
                     qn_ref, qp_ref, kn_ref, kp_ref, mq_ref, cos_ref, sin_ref,
                     qt_ref, k_ref, vt_ref, qm_ref, *, qk_scale, mem_scale):
    h = (_rms(x_ref[...]) * g_ref[...]).astype(BF16)
    proj = jnp.dot(h, w_in_ref[...], preferred_element_type=F32)
    o_kv = MLA_Q_RANK
    o_pe = o_kv + MLA_KV_RANK
    o_qm = o_pe + HEAD
    c_q = (_rms(proj[:, :o_kv]) * qa_ref[...]).astype(BF16)
    c_kv = (_rms(proj[:, o_kv:o_pe]) * kva_ref[...]).astype(BF16)
    q = jnp.dot(c_q, w_qb_ref[...], preferred_element_type=F32)
    kv = jnp.dot(c_kv, w_kvb_ref[...], preferred_element_type=F32)
    cos = cos_ref[...]
    sin = sin_ref[...]
    half = MLA_ROPE // 4
    k_pe = _rope(_rms(proj[:, o_pe:o_qm], MLA_ROPE) * kp_ref[...], cos, sin, half).astype(BF16)
    n_w = MLA_HEADS * HEAD
    for i in range(MLA_HEADS):
        lo = i * HEAD
        q_n = _rms(q[:, lo:lo + HEAD]) * qn_ref[...] * qk_scale
        q_p = _rope(_rms(q[:, n_w + lo:n_w + lo + HEAD], MLA_ROPE) * qp_ref[...], cos, sin, half)
        qt_ref[i, :HEAD, :] = q_n.T.astype(BF16)
        qt_ref[i, HEAD:, :] = (q_p * qk_scale).T.astype(BF16)
        k_n = _rms(kv[:, lo:lo + HEAD]) * kn_ref[...]
        k_ref[:, 2 * lo:2 * lo + HEAD] = k_n.astype(BF16)
        k_ref[:, 2 * lo + HEAD:2 * lo + 2 * HEAD] = k_pe
    for i in range(MLA_HEADS):
        lo = n_w + i * HEAD
        _store_vt(vt_ref, i, kv[:, lo:lo + HEAD])
    for i in range(MEM_HEADS):
        lo = i * HEAD
        q_m = _rms(proj[:, o_qm + lo:o_qm + lo + HEAD]) * mq_ref[...] * mem_scale
        qm_ref[:, lo:lo + HEAD] = q_m.astype(BF16)


def _proj_gqa_kernel(x_ref, g_ref, w_in_ref, qn_ref, kn_ref, mq_ref, cos_ref, sin_ref,
                     qt_ref, k_ref, vt_ref, qm_ref, *, qk_scale, mem_scale):
    h = (_rms(x_ref[...]) * g_ref[...]).astype(BF16)
    proj = jnp.dot(h, w_in_ref[...], preferred_element_type=F32)
    cos = cos_ref[...]
    sin = sin_ref[...]
    half = HEAD // 4
    o_k = GQA_HEADS * HEAD
    o_v = o_k + GQA_KV_HEADS * HEAD
    o_qm = o_v + GQA_KV_HEADS * HEAD
    for i in range(GQA_HEADS):
        lo = i * HEAD
        q_h = _rope(_rms(proj[:, lo:lo + HEAD]) * qn_ref[...], cos, sin, half) * qk_scale
        qt_ref[i] = q_h.T.astype(BF16)
    for i in range(GQA_KV_HEADS):
        lo = i * HEAD
        k_h = _rope(_rms(proj[:, o_k + lo:o_k + lo + HEAD]) * kn_ref[...], cos, sin, half)
        k_ref[:, lo:lo + HEAD] = k_h.astype(BF16)
    for i in range(GQA_KV_HEADS):
        lo = o_v + i * HEAD
        _store_vt(vt_ref, i, proj[:, lo:lo + HEAD])
    for i in range(MEM_HEADS):
        lo = i * HEAD
        q_m = _rms(proj[:, o_qm + lo:o_qm + lo + HEAD]) * mq_ref[...] * mem_scale
        qm_ref[:, lo:lo + HEAD] = q_m.astype(BF16)


def _row_tile(n_rows, seq, want):
    t = min(want, seq)
    assert seq % t == 0 and n_rows % t == 0
    return t


def _full(shape):
    return pl.BlockSpec(shape, lambda i: (0,) * len(shape))


def _proj_call(kernel, x2, seq, consts, tables, n_heads, dk, n_kv_heads, tm, tk):
    n_rows = x2.shape[0]
    batch = n_rows // seq
    n_pos = seq // tm
    row = lambda w: pl.BlockSpec((tm, w), lambda i: (i, 0))
    tab = pl.BlockSpec((tm, HEAD), lambda i: (i % n_pos, 0))
    rows_out = lambda w: jax.ShapeDtypeStruct((n_rows, w), BF16)
    qt_spec = pl.BlockSpec((None, n_heads, dk, tm), lambda i: (i // n_pos, 0, 0, i % n_pos))
    qt_out = jax.ShapeDtypeStruct((batch, n_heads, dk, seq), BF16)
    per_chunk = tk // tm
    vt_spec = pl.BlockSpec(
        (None, n_kv_heads, None, HEAD + SUM_ROWS, tm),
        lambda i: (i // n_pos, 0, (i % n_pos) // per_chunk, 0, (i % n_pos) % per_chunk))
    vt_out = jax.ShapeDtypeStruct((batch, n_kv_heads, seq // tk, HEAD + SUM_ROWS, tk), BF16)
    return pl.pallas_call(
        kernel,
        grid=(n_rows // tm,),
        in_specs=[row(D_MODEL)] + [_full(c.shape) for c in consts] + [tab, tab],
        out_specs=[qt_spec, row(n_kv_heads * dk), vt_spec, row(MEM_W)],
        out_shape=[qt_out, rows_out(n_kv_heads * dk), vt_out, rows_out(MEM_W)],
        compiler_params=_params(1),
        name=kernel.func.__name__.strip("_"),
    )(x2, *consts, *tables)


def _attn_kernel(qt_ref, k_ref, vt_ref, o_ref, s_ref, mx_ref, p_ref, a_ref, m_ref, acc_ref, *, tk):
    n_chunks = vt_ref.shape[0]
    dv = o_ref.shape[1]
    tq = qt_ref.shape[1]
    col_tiles = [slice(c, c + COL_TILE) for c in range(0, tq, COL_TILE)]

    def scores(c, slot, cols):
        start = pl.multiple_of(c * tk, tk)
        s = jnp.dot(k_ref[pl.ds(start, tk), :], qt_ref[:, cols], preferred_element_type=F32)
        s_ref[slot, :, cols] = s
        mx_ref[slot, :, cols] = jnp.max(s.reshape(tk // 8, 8, COL_TILE), axis=0)

    def softmax(slot, cols):
        m_old = m_ref[:, cols]
        m_new = jnp.maximum(m_old, jnp.max(mx_ref[slot, :, cols], axis=0, keepdims=True))
        m_ref[:, cols] = m_new
        a_ref[slot, :, cols] = jnp.exp2(m_old - m_new)
        for r in range(0, tk, EXP_ROWS):
            p = jnp.exp2(s_ref[slot, r:r + EXP_ROWS, cols] - m_new)
            p_ref[slot, r:r + EXP_ROWS, cols] = p.astype(BF16)

    def accumulate(c, slot, cols):
        pv = jnp.dot(vt_ref[c], p_ref[slot, :, cols], preferred_element_type=F32)
        acc_ref[:, cols] = acc_ref[:, cols] * a_ref[slot, :, cols] + pv

    def stages(c_acc, c_scores, cur):
        for cols in col_tiles:
            softmax(cur, cols)
            scores(c_scores, 1 - cur, cols)
            accumulate(c_acc, 1 - cur, cols)

    m_ref[...] = jnp.full(m_ref.shape, -jnp.inf, F32)
    acc_ref[...] = jnp.zeros(acc_ref.shape, F32)
    p_ref[1] = jnp.zeros(p_ref.shape[1:], BF16)
    a_ref[1] = jnp.ones(a_ref.shape[1:], F32)
    for cols in col_tiles:
        scores(0, 0, cols)

    def pair(i, carry):
        j = 2 * i
        stages(jnp.maximum(j - 1, 0), j + 1, 0)
        stages(j, jnp.minimum(j + 2, n_chunks - 1), 1)
        return carry

    lax.fori_loop(0, n_chunks // 2, pair, 0)
    for cols in col_tiles:
        accumulate(n_chunks - 1, 1, cols)
    o_ref[...] = (acc_ref[:dv, :] / acc_ref[dv:dv + 1, :]).T.astype(o_ref.dtype)


def _attention(qt, k, vt, tq):
    batch, n_heads, dk, seq = qt.shape
    _, n_kv_heads, n_chunks, dv_ext, tk = vt.shape
    dv = dv_ext - SUM_ROWS
    group = n_heads // n_kv_heads
    tq = min(tq, seq)
    return pl.pallas_call(
        functools.partial(_attn_kernel, tk=tk),
        grid=(batch, n_heads, seq // tq),
        in_specs=[
            pl.BlockSpec((None, None, dk, tq), lambda b, h, i: (b, h, 0, i)),
            pl.BlockSpec((None, seq, dk), lambda b, h, i: (b, 0, h // group)),
            pl.BlockSpec((None, None, n_chunks, dv_ext, tk),
                         lambda b, h, i: (b, h // group, 0, 0, 0)),
        ],
        out_specs=pl.BlockSpec((None, tq, dv), lambda b, h, i: (b, i, h)),
        out_shape=jax.ShapeDtypeStruct((batch, seq, n_heads * dv), BF16),
        scratch_shapes=[
            pltpu.VMEM((2, tk, tq), F32),
            pltpu.VMEM((2, 8, tq), F32),
            pltpu.VMEM((2, tk, tq), BF16),
            pltpu.VMEM((2, 1, tq), F32),
            pltpu.VMEM((1, tq), F32),
            pltpu.VMEM((dv_ext, tq), F32),
        ],
        compiler_params=_params(3),
        name="attention",
    )(qt, k, vt)


def _post_kernel(x_ref, mix_ref, qm_ref, km_ref, vm_ref, w_ref, o_ref):
    n_mix = mix_ref.shape[-1]
    y = x_ref[...] + jnp.dot(mix_ref[...], w_ref[:n_mix, :], preferred_element_type=F32)
    for i in range(MEM_HEADS):
        lo = i * HEAD
        s = _nt_dot(qm_ref[:, lo:lo + HEAD], km_ref[:, lo:lo + HEAD])
        e = jnp.exp2(s - jnp.max(s, axis=-1, keepdims=True))
        pv = jnp.dot(e.astype(BF16), vm_ref[:, lo:lo + HEAD], preferred_element_type=F32)
        o_mem = (pv / jnp.sum(e, axis=-1, keepdims=True)).astype(BF16)
        y = y + jnp.dot(o_mem, w_ref[n_mix + lo:n_mix + lo + HEAD, :], preferred_element_type=F32)
    o_ref[...] = y


def _post(x2, mix2, qm2, k_mem, v_mem, w_out, seq, tm):
    n_rows = x2.shape[0]
    per_batch = seq // tm
    n_mem = k_mem.shape[1]
    row = lambda w: pl.BlockSpec((tm, w), lambda i: (i, 0))
    memspec = pl.BlockSpec((None, n_mem, MEM_W), lambda i: (i // per_batch, 0, 0))
    return pl.pallas_call(
        _post_kernel,
        grid=(n_rows // tm,),
        in_specs=[row(D_MODEL), row(mix2.shape[1]), row(MEM_W), memspec, memspec,
                  _full(w_out.shape)],
        out_specs=row(D_MODEL),
        out_shape=jax.ShapeDtypeStruct((n_rows, D_MODEL), F32),
        compiler_params=_params(1),
        name="mix_out",
    )(x2, mix2, qm2, k_mem, v_mem, w_out)


def _ffn_kernel(x_ref, g_ref, w_gu_ref, w_down_ref, o_ref):
    x = x_ref[...]
    h = (_rms(x) * g_ref[...]).astype(BF16)
    y = x
    for c in range(D_FF // FF_CHUNK):
        lo = c * FF_CHUNK
        gate = jnp.dot(h, w_gu_ref[:, lo:lo + FF_CHUNK], preferred_element_type=F32)
        up = jnp.dot(h, w_gu_ref[:, D_FF + lo:D_FF + lo + FF_CHUNK], preferred_element_type=F32)
        act = (gate * jax.nn.sigmoid(gate) * up).astype(BF16)
        y = y + jnp.dot(act, w_down_ref[lo:lo + FF_CHUNK, :], preferred_element_type=F32)
    o_ref[...] = y


def _ffn(x2, g, w_gu, w_down, tm):
    n_rows = x2.shape[0]
    row = pl.BlockSpec((tm, D_MODEL), lambda i: (i, 0))
    return pl.pallas_call(
        _ffn_kernel,
        grid=(n_rows // tm,),
        in_specs=[row, _full((1, D_MODEL)), _full(w_gu.shape), _full(w_down.shape)],
        out_specs=row,
        out_shape=jax.ShapeDtypeStruct((n_rows, D_MODEL), F32),
        compiler_params=_params(1),
        name="ffn",
    )(x2, g.reshape(1, D_MODEL), w_gu, w_down)


def _rope_tables(seq, dim):
    rows = seq // GRID_W
    row = jnp.repeat(jnp.arange(rows, dtype=F32), GRID_W)
    col = jnp.tile(jnp.arange(GRID_W, dtype=F32), rows)
    axis_dim = dim // 2
    inv = ROPE_THETA ** (-jnp.arange(0, axis_dim, 2, dtype=F32) / axis_dim)
    ang_r = row[:, None] * inv
    ang_c = col[:, None] * inv
    cos = jnp.concatenate([jnp.cos(ang_r)] * 2 + [jnp.cos(ang_c)] * 2, axis=-1)
    sin = jnp.concatenate([-jnp.sin(ang_r), jnp.sin(ang_r), -jnp.sin(ang_c), jnp.sin(ang_c)], axis=-1)
    pad = ((0, 0), (0, HEAD - dim))
    return jnp.pad(cos, pad), jnp.pad(sin, pad)


def _pad_lanes(v, width=HEAD):
    return jnp.pad(v, (0, width - v.shape[0])).reshape(1, width)


def kernel(x, mem, mem_norm, norm_mix, norm_ffn, w_out, w_mem_kv, memq_norm, memk_norm, w_gate_up, w_down, mla_w_in, mla_q_a_norm, mla_w_q_b, mla_kv_a_norm, mla_w_kv_b, mla_q_norm, mla_k_norm, gqa_w_in, gqa_q_norm, gqa_k_norm):
    batch, seq, _ = x.shape
    depth = norm_mix.shape[0]
    n_rows = batch * seq
    tm_proj = _row_tile(n_rows, seq, 512)
    tk = _row_tile(n_rows, seq, 512)
    tm_post = _row_tile(n_rows, seq, 512)
    tm_ffn = _row_tile(n_rows, seq, 512)
    mem_scale = HEAD ** -0.5 * LOG2E

    k_mem, v_mem = _memkv(mem, mem_norm, w_mem_kv, memk_norm)
    x2 = x.reshape(n_rows, D_MODEL)
    for i in range(depth):
        j = i // 2
        g_mix = norm_mix[i].reshape(1, D_MODEL)
        mq = memq_norm[i].reshape(1, HEAD)
        if i % 2 == 0:
            w_in = mla_w_in[j]
            o_qm = MLA_Q_RANK + MLA_KV_RANK + MLA_ROPE
            w_in = jnp.concatenate(
                [w_in[:, :o_qm], jnp.zeros((D_MODEL, HEAD - MLA_ROPE), F32), w_in[:, o_qm:]], axis=1)
            w_qb = mla_w_q_b[j].reshape(MLA_Q_RANK, MLA_HEADS, MLA_NOPE + MLA_ROPE)
            w_qb = jnp.concatenate(
                [w_qb[:, :, :MLA_NOPE].reshape(MLA_Q_RANK, -1),
                 jnp.pad(w_qb[:, :, MLA_NOPE:], ((0, 0), (0, 0), (0, HEAD - MLA_ROPE))
                         ).reshape(MLA_Q_RANK, -1)], axis=1)
            w_kvb = mla_w_kv_b[j].reshape(MLA_KV_RANK, MLA_HEADS, 2 * HEAD)
            w_kvb = jnp.concatenate([w_kvb[:, :, :HEAD].reshape(MLA_KV_RANK, -1),
                                     w_kvb[:, :, HEAD:].reshape(MLA_KV_RANK, -1)], axis=1)
            consts = [g_mix, w_in.astype(BF16),
                      mla_q_a_norm[j].reshape(1, -1), w_qb.astype(BF16),
                      mla_kv_a_norm[j].reshape(1, -1), w_kvb.astype(BF16),
                      _pad_lanes(mla_q_norm[j][:MLA_NOPE]), _pad_lanes(mla_q_norm[j][MLA_NOPE:]),
                      _pad_lanes(mla_k_norm[j][:MLA_NOPE]), _pad_lanes(mla_k_norm[j][MLA_NOPE:]),
                      mq]
            kern = functools.partial(_proj_mla_kernel, mem_scale=mem_scale,
                                     qk_scale=(MLA_NOPE + MLA_ROPE) ** -0.5 * LOG2E)
            qt, k2, vt, qm2 = _proj_call(
                kern, x2, seq, consts, _rope_tables(seq, MLA_ROPE),
                MLA_HEADS, MLA_QK, MLA_HEADS, tm_proj, tk)
        else:
            consts = [g_mix, gqa_w_in[j].astype(BF16),
                      gqa_q_norm[j].reshape(1, HEAD), gqa_k_norm[j].reshape(1, HEAD), mq]
            kern = functools.partial(_proj_gqa_kernel, mem_scale=mem_scale,
                                     qk_scale=HEAD ** -0.5 * LOG2E)
            qt, k2, vt, qm2 = _proj_call(
                kern, x2, seq, consts, _rope_tables(seq, HEAD),
                GQA_HEADS, HEAD, GQA_KV_HEADS, tm_proj, tk)
        mix = _attention(qt, k2.reshape(batch, seq, -1), vt, tq=1024)
        x2 = _post(x2, mix.reshape(n_rows, -1), qm2, k_mem[i], v_mem[i],
                   w_out[i].astype(BF16), seq, tm_post)
        x2 = _ffn(x2, norm_ffn[i], w_gate_up[i].astype(BF16), w_down[i].astype(BF16), tm_ffn)
    return x2.reshape(batch, seq, D_MODEL)
```

```python
import functools

import jax
import jax.numpy as jnp
from jax import lax
from jax.experimental import pallas as pl
from jax.experimental.pallas import tpu as pltpu

D_MODEL = 1024
GRID_W = 64
ROPE_THETA = 10000.0
EPS = 1e-6
LOG2E = 1.4426950408889634

MEM_HEADS = 4
HEAD = 128
MEM_W = MEM_HEADS * HEAD

MLA_HEADS = 8
MLA_Q_RANK = 384
MLA_KV_RANK = 256
MLA_NOPE = 128
MLA_ROPE = 64
MLA_QK = 2 * HEAD

GQA_HEADS = 8
GQA_KV_HEADS = 2

D_FF = 2816
FF_CHUNK = 256
COL_TILE = 256
EXP_ROWS = 64
SUM_ROWS = 16

VMEM_LIMIT = 56 * 1024 * 1024

BF16 = jnp.bfloat16
F32 = jnp.float32


def _params(n_axes):
    return pltpu.CompilerParams(dimension_semantics=("arbitrary",) * n_axes,
                                vmem_limit_bytes=VMEM_LIMIT)


def _rms(x, width=None):
    n = x.shape[-1] if width is None else width
    ms = jnp.sum(x * x, axis=-1, keepdims=True) * (1.0 / n)
    return x * lax.rsqrt(ms + EPS)


def _rope(x, cos, sin, half):
    lane = lax.broadcasted_iota(jnp.int32, x.shape, 1)
    upper = pltpu.roll(x, HEAD - half, axis=1)
    lower = pltpu.roll(x, half, axis=1)
    swapped = jnp.where((lane & (2 * half - 1)) < half, upper, lower)
    return x * cos + swapped * sin


def _nt_dot(a, b):
    return lax.dot_general(a, b, (((1,), (1,)), ((), ())), preferred_element_type=F32)


def _memkv_kernel(mem_ref, g_ref, w_ref, kn_ref, k_ref, v_ref):
    mem_n = (_rms(mem_ref[...]) * g_ref[...]).astype(BF16)
    kv = jnp.dot(mem_n, w_ref[...], preferred_element_type=F32)
    for h in range(MEM_HEADS):
        kh = _rms(kv[:, h * HEAD:(h + 1) * HEAD]) * kn_ref[...]
        k_ref[:, h * HEAD:(h + 1) * HEAD] = kh.astype(BF16)
    v_ref[...] = kv[:, MEM_W:].astype(BF16)


def _memkv(mem, mem_norm, w_mem_kv, memk_norm):
    depth = w_mem_kv.shape[0]
    batch, n_mem, _ = mem.shape
    out = jax.ShapeDtypeStruct((depth, batch, n_mem, MEM_W), BF16)
    return pl.pallas_call(
        _memkv_kernel,
        grid=(depth, batch),
        in_specs=[
            pl.BlockSpec((None, n_mem, D_MODEL), lambda l, b: (b, 0, 0)),
            pl.BlockSpec((1, D_MODEL), lambda l, b: (0, 0)),
            pl.BlockSpec((None, D_MODEL, 2 * MEM_W), lambda l, b: (l, 0, 0)),
            pl.BlockSpec((None, 1, HEAD), lambda l, b: (l, 0, 0)),
        ],
        out_specs=[
            pl.BlockSpec((None, None, n_mem, MEM_W), lambda l, b: (l, b, 0, 0)),
            pl.BlockSpec((None, None, n_mem, MEM_W), lambda l, b: (l, b, 0, 0)),
        ],
        out_shape=[out, out],
        compiler_params=_params(2),
        name="mem_kv",
    )(mem, mem_norm.reshape(1, D_MODEL), w_mem_kv.astype(BF16),
      memk_norm.reshape(depth, 1, HEAD))


def _store_vt(vt_ref, head, v):
    tm = v.shape[0]
    vt_ref[head, :HEAD, :] = v.T.astype(BF16)
    row = lax.broadcasted_iota(jnp.int32, (SUM_ROWS, tm), 0)
    vt_ref[head, HEAD:, :] = jnp.where(row == 0, 1.0, 0.0).astype(BF16)


def _proj_mla_kernel(x_ref, g_ref, w_in_ref, qa_ref, w_qb_ref, kva_ref, w_kvb_ref,
                     qn_ref, qp_ref, kn_ref, kp_ref, mq_ref, cos_ref, sin_ref,
                     qt_ref, k_ref, vt_ref, qm_ref, *, qk_scale, mem_scale):
    h = (_rms(x_ref[...]) * g_ref[...]).astype(BF16)
    proj = jnp.dot(h, w_in_ref[...], preferred_element_type=F32)
    o_kv = MLA_Q_RANK
    o_pe = o_kv + MLA_KV_RANK
    o_qm = o_pe + HEAD
    c_q = (_rms(proj[:, :o_kv]) * qa_ref[...]).astype(BF16)
    c_kv = (_rms(proj[:, o_kv:o_pe]) * kva_ref[...]).astype(BF16)
    q = jnp.dot(c_q, w_qb_ref[...], preferred_element_type=F32)
    kv = jnp.dot(c_kv, w_kvb_ref[...], preferred_element_type=F32)
    cos = cos_ref[...]
    sin = sin_ref[...]
    half = MLA_ROPE // 4
    k_pe = _rope(_rms(proj[:, o_pe:o_qm], MLA_ROPE) * kp_ref[...], cos, sin, half).astype(BF16)
    n_w = MLA_HEADS * HEAD
    for i in range(MLA_HEADS):
        lo = i * HEAD
        q_n = _rms(q[:, lo:lo + HEAD]) * qn_ref[...] * qk_scale
        q_p = _rope(_rms(q[:, n_w + lo:n_w + lo + HEAD], MLA_ROPE) * qp_ref[...], cos, sin, half)
        qt_ref[i, :HEAD, :] = q_n.T.astype(BF16)
        qt_ref[i, HEAD:, :] = (q_p * qk_scale).T.astype(BF16)
        k_n = _rms(kv[:, lo:lo + HEAD]) * kn_ref[...]
        k_ref[:, 2 * lo:2 * lo + HEAD] = k_n.astype(BF16)
        k_ref[:, 2 * lo + HEAD:2 * lo + 2 * HEAD] = k_pe
    for i in range(MLA_HEADS):
        lo = n_w + i * HEAD
        _store_vt(vt_ref, i, kv[:, lo:lo + HEAD])
    for i in range(MEM_HEADS):
        lo = i * HEAD
        q_m = _rms(proj[:, o_qm + lo:o_qm + lo + HEAD]) * mq_ref[...] * mem_scale
        qm_ref[:, lo:lo + HEAD] = q_m.astype(BF16)


def _proj_gqa_kernel(x_ref, g_ref, w_in_ref, qn_ref, kn_ref, mq_ref, cos_ref, sin_ref,
                     qt_ref, k_ref, vt_ref, qm_ref, *, qk_scale, mem_scale):
    h = (_rms(x_ref[...]) * g_ref[...]).astype(BF16)
    proj = jnp.dot(h, w_in_ref[...], preferred_element_type=F32)
    cos = cos_ref[...]
    sin = sin_ref[...]
    half = HEAD // 4
    o_k = GQA_HEADS * HEAD
    o_v = o_k + GQA_KV_HEADS * HEAD
    o_qm = o_v + GQA_KV_HEADS * HEAD
    for i in range(GQA_HEADS):
        lo = i * HEAD
        q_h = _rope(_rms(proj[:, lo:lo + HEAD]) * qn_ref[...], cos, sin, half) * qk_scale
        qt_ref[i] = q_h.T.astype(BF16)
    for i in range(GQA_KV_HEADS):
        lo = i * HEAD
        k_h = _rope(_rms(proj[:, o_k + lo:o_k + lo + HEAD]) * kn_ref[...], cos, sin, half)
        k_ref[:, lo:lo + HEAD] = k_h.astype(BF16)
    for i in range(GQA_KV_HEADS):
        lo = o_v + i * HEAD
        _store_vt(vt_ref, i, proj[:, lo:lo + HEAD])
    for i in range(MEM_HEADS):
        lo = i * HEAD
        q_m = _rms(proj[:, o_qm + lo:o_qm + lo + HEAD]) * mq_ref[...] * mem_scale
        qm_ref[:, lo:lo + HEAD] = q_m.astype(BF16)


def _row_tile(n_rows, seq, want):
    t = min(want, seq)
    assert seq % t == 0 and n_rows % t == 0
    return t


def _full(shape):
    return pl.BlockSpec(shape, lambda i: (0,) * len(shape))


def _proj_call(kernel, x2, seq, consts, tables, n_heads, dk, n_kv_heads, tm, tk):
    n_rows = x2.shape[0]
    batch = n_rows // seq
    n_pos = seq // tm
    row = lambda w: pl.BlockSpec((tm, w), lambda i: (i, 0))
    tab = pl.BlockSpec((tm, HEAD), lambda i: (i % n_pos, 0))
    rows_out = lambda w: jax.ShapeDtypeStruct((n_rows, w), BF16)
    qt_spec = pl.BlockSpec((None, n_heads, dk, tm), lambda i: (i // n_pos, 0, 0, i % n_pos))
    qt_out = jax.ShapeDtypeStruct((batch, n_heads, dk, seq), BF16)
    per_chunk = tk // tm
    vt_spec = pl.BlockSpec(
        (None, n_kv_heads, None, HEAD + SUM_ROWS, tm),
        lambda i: (i // n_pos, 0, (i % n_pos) // per_chunk, 0, (i % n_pos) % per_chunk))
    vt_out = jax.ShapeDtypeStruct((batch, n_kv_heads, seq // tk, HEAD + SUM_ROWS, tk), BF16)
    return pl.pallas_call(
        kernel,
        grid=(n_rows // tm,),
        in_specs=[row(D_MODEL)] + [_full(c.shape) for c in consts] + [tab, tab],
        out_specs=[qt_spec, row(n_kv_heads * dk), vt_spec, row(MEM_W)],
        out_shape=[qt_out, rows_out(n_kv_heads * dk), vt_out, rows_out(MEM_W)],
        compiler_params=_params(1),
        name=kernel.func.__name__.strip("_"),
    )(x2, *consts, *tables)


def _attn_kernel(qt_ref, k_ref, vt_ref, o_ref, s_ref, mx_ref, p_ref, a_ref, m_ref, acc_ref, *, tk):
    n_chunks = vt_ref.shape[0]
    dv = o_ref.shape[1]
    tq = qt_ref.shape[1]
    col_tiles = [slice(c, c + COL_TILE) for c in range(0, tq, COL_TILE)]

    def scores(c, slot, cols):
        start = pl.multiple_of(c * tk, tk)
        s = jnp.dot(k_ref[pl.ds(start, tk), :], qt_ref[:, cols], preferred_element_type=F32)
        s_ref[slot, :, cols] = s
        mx_ref[slot, :, cols] = jnp.max(s.reshape(tk // 8, 8, COL_TILE), axis=0)

    def softmax(slot, cols):
        m_old = m_ref[:, cols]
        m_new = jnp.maximum(m_old, jnp.max(mx_ref[slot, :, cols], axis=0, keepdims=True))
        m_ref[:, cols] = m_new
        a_ref[slot, :, cols] = jnp.exp2(m_old - m_new)
        for r in range(0, tk, EXP_ROWS):
            p = jnp.exp2(s_ref[slot, r:r + EXP_ROWS, cols] - m_new)
            p_ref[slot, r:r + EXP_ROWS, cols] = p.astype(BF16)

    def accumulate(c, slot, cols):
        pv = jnp.dot(vt_ref[c], p_ref[slot, :, cols], preferred_element_type=F32)
        acc_ref[:, cols] = acc_ref[:, cols] * a_ref[slot, :, cols] + pv

    def stages(c_acc, c_scores, cur):
        for cols in col_tiles:
            softmax(cur, cols)
            scores(c_scores, 1 - cur, cols)
            accumulate(c_acc, 1 - cur, cols)

    m_ref[...] = jnp.full(m_ref.shape, -jnp.inf, F32)
    acc_ref[...] = jnp.zeros(acc_ref.shape, F32)
    p_ref[1] = jnp.zeros(p_ref.shape[1:], BF16)
    a_ref[1] = jnp.ones(a_ref.shape[1:], F32)
    for cols in col_tiles:
        scores(0, 0, cols)

    def pair(i, carry):
        j = 2 * i
        stages(jnp.maximum(j - 1, 0), j + 1, 0)
        stages(j, jnp.minimum(j + 2, n_chunks - 1), 1)
        return carry

    lax.fori_loop(0, n_chunks // 2, pair, 0)
    for cols in col_tiles:
        accumulate(n_chunks - 1, 1, cols)
    o_ref[...] = (acc_ref[:dv, :] / acc_ref[dv:dv + 1, :]).T.astype(o_ref.dtype)


def _attention(qt, k, vt, tq):
    batch, n_heads, dk, seq = qt.shape
    _, n_kv_heads, n_chunks, dv_ext, tk = vt.shape
    dv = dv_ext - SUM_ROWS
    group = n_heads // n_kv_heads
    tq = min(tq, seq)
    return pl.pallas_call(
        functools.partial(_attn_kernel, tk=tk),
        grid=(batch, n_heads, seq // tq),
        in_specs=[
            pl.BlockSpec((None, None, dk, tq), lambda b, h, i: (b, h, 0, i)),
            pl.BlockSpec((None, seq, dk), lambda b, h, i: (b, 0, h // group)),
            pl.BlockSpec((None, None, n_chunks, dv_ext, tk),
                         lambda b, h, i: (b, h // group, 0, 0, 0)),
        ],
        out_specs=pl.BlockSpec((None, tq, dv), lambda b, h, i: (b, i, h)),
        out_shape=jax.ShapeDtypeStruct((batch, seq, n_heads * dv), BF16),
        scratch_shapes=[
            pltpu.VMEM((2, tk, tq), F32),
            pltpu.VMEM((2, 8, tq), F32),
            pltpu.VMEM((2, tk, tq), BF16),
            pltpu.VMEM((2, 1, tq), F32),
            pltpu.VMEM((1, tq), F32),
            pltpu.VMEM((dv_ext, tq), F32),
        ],
        compiler_params=_params(3),
        name="attention",
    )(qt, k, vt)


def _post_kernel(x_ref, mix_ref, qm_ref, km_ref, vm_ref, w_ref, o_ref):
    n_mix = mix_ref.shape[-1]
    y = x_ref[...] + jnp.dot(mix_ref[...], w_ref[:n_mix, :], preferred_element_type=F32)
    for i in range(MEM_HEADS):
        lo = i * HEAD
        s = _nt_dot(qm_ref[:, lo:lo + HEAD], km_ref[:, lo:lo + HEAD])
        e = jnp.exp2(s - jnp.max(s, axis=-1, keepdims=True))
        pv = jnp.dot(e.astype(BF16), vm_ref[:, lo:lo + HEAD], preferred_element_type=F32)
        o_mem = (pv / jnp.sum(e, axis=-1, keepdims=True)).astype(BF16)
        y = y + jnp.dot(o_mem, w_ref[n_mix + lo:n_mix + lo + HEAD, :], preferred_element_type=F32)
    o_ref[...] = y


def _post(x2, mix2, qm2, k_mem, v_mem, w_out, seq, tm):
    n_rows = x2.shape[0]
    per_batch = seq // tm
    n_mem = k_mem.shape[1]
    row = lambda w: pl.BlockSpec((tm, w), lambda i: (i, 0))
    memspec = pl.BlockSpec((None, n_mem, MEM_W), lambda i: (i // per_batch, 0, 0))
    return pl.pallas_call(
        _post_kernel,
        grid=(n_rows // tm,),
        in_specs=[row(D_MODEL), row(mix2.shape[1]), row(MEM_W), memspec, memspec,
                  _full(w_out.shape)],
        out_specs=row(D_MODEL),
        out_shape=jax.ShapeDtypeStruct((n_rows, D_MODEL), F32),
        compiler_params=_params(1),
        name="mix_out",
    )(x2, mix2, qm2, k_mem, v_mem, w_out)


def _ffn_kernel(x_ref, g_ref, w_gu_ref, w_down_ref, o_ref):
    x = x_ref[...]
    h = (_rms(x) * g_ref[...]).astype(BF16)
    y = x
    for c in range(D_FF // FF_CHUNK):
        lo = c * FF_CHUNK
        gate = jnp.dot(h, w_gu_ref[:, lo:lo + FF_CHUNK], preferred_element_type=F32)
        up = jnp.dot(h, w_gu_ref[:, D_FF + lo:D_FF + lo + FF_CHUNK], preferred_element_type=F32)
        act = (gate * jax.nn.sigmoid(gate) * up).astype(BF16)
        y = y + jnp.dot(act, w_down_ref[lo:lo + FF_CHUNK, :], preferred_element_type=F32)
    o_ref[...] = y


def _ffn(x2, g, w_gu, w_down, tm):
    n_rows = x2.shape[0]
    row = pl.BlockSpec((tm, D_MODEL), lambda i: (i, 0))
    return pl.pallas_call(
        _ffn_kernel,
        grid=(n_rows // tm,),
        in_specs=[row, _full((1, D_MODEL)), _full(w_gu.shape), _full(w_down.shape)],
        out_specs=row,
        out_shape=jax.ShapeDtypeStruct((n_rows, D_MODEL), F32),
        compiler_params=_params(1),
        name="ffn",
    )(x2, g.reshape(1, D_MODEL), w_gu, w_down)


def _rope_tables(seq, dim):
    rows = seq // GRID_W
    row = jnp.repeat(jnp.arange(rows, dtype=F32), GRID_W)
    col = jnp.tile(jnp.arange(GRID_W, dtype=F32), rows)
    axis_dim = dim // 2
    inv = ROPE_THETA ** (-jnp.arange(0, axis_dim, 2, dtype=F32) / axis_dim)
    ang_r = row[:, None] * inv
    ang_c = col[:, None] * inv
    cos = jnp.concatenate([jnp.cos(ang_r)] * 2 + [jnp.cos(ang_c)] * 2, axis=-1)
    sin = jnp.concatenate([-jnp.sin(ang_r), jnp.sin(ang_r), -jnp.sin(ang_c), jnp.sin(ang_c)], axis=-1)
    pad = ((0, 0), (0, HEAD - dim))
    return jnp.pad(cos, pad), jnp.pad(sin, pad)


def _pad_lanes(v, width=HEAD):
    return jnp.pad(v, (0, width - v.shape[0])).reshape(1, width)


def kernel(x, mem, mem_norm, norm_mix, norm_ffn, w_out, w_mem_kv, memq_norm, memk_norm, w_gate_up, w_down, mla_w_in, mla_q_a_norm, mla_w_q_b, mla_kv_a_norm, mla_w_kv_b, mla_q_norm, mla_k_norm, gqa_w_in, gqa_q_norm, gqa_k_norm):
    batch, seq, _ = x.shape
    depth = norm_mix.shape[0]
    n_rows = batch * seq
    tm_proj = _row_tile(n_rows, seq, 512)
    tk = _row_tile(n_rows, seq, 512)
    tm_post = _row_tile(n_rows, seq, 512)
    tm_ffn = _row_tile(n_rows, seq, 512)
    mem_scale = HEAD ** -0.5 * LOG2E

    k_mem, v_mem = _memkv(mem, mem_norm, w_mem_kv, memk_norm)
    x2 = x.reshape(n_rows, D_MODEL)
    for i in range(depth):
        j = i // 2
        g_mix = norm_mix[i].reshape(1, D_MODEL)
        mq = memq_norm[i].reshape(1, HEAD)
        if i % 2 == 0:
            w_in = mla_w_in[j]
            o_qm = MLA_Q_RANK + MLA_KV_RANK + MLA_ROPE
            w_in = jnp.concatenate(
                [w_in[:, :o_qm], jnp.zeros((D_MODEL, HEAD - MLA_ROPE), F32), w_in[:, o_qm:]], axis=1)
            w_qb = mla_w_q_b[j].reshape(MLA_Q_RANK, MLA_HEADS, MLA_NOPE + MLA_ROPE)
            w_qb = jnp.concatenate(
                [w_qb[:, :, :MLA_NOPE].reshape(MLA_Q_RANK, -1),
                 jnp.pad(w_qb[:, :, MLA_NOPE:], ((0, 0), (0, 0), (0, HEAD - MLA_ROPE))
                         ).reshape(MLA_Q_RANK, -1)], axis=1)
            w_kvb = mla_w_kv_b[j].reshape(MLA_KV_RANK, MLA_HEADS, 2 * HEAD)
            w_kvb = jnp.concatenate([w_kvb[:, :, :HEAD].reshape(MLA_KV_RANK, -1),
                                     w_kvb[:, :, HEAD:].reshape(MLA_KV_RANK, -1)], axis=1)
            consts = [g_mix, w_in.astype(BF16),
                      mla_q_a_norm[j].reshape(1, -1), w_qb.astype(BF16),
                      mla_kv_a_norm[j].reshape(1, -1), w_kvb.astype(BF16),
                      _pad_lanes(mla_q_norm[j][:MLA_NOPE]), _pad_lanes(mla_q_norm[j][MLA_NOPE:]),
                      _pad_lanes(mla_k_norm[j][:MLA_NOPE]), _pad_lanes(mla_k_norm[j][MLA_NOPE:]),
                      mq]
            kern = functools.partial(_proj_mla_kernel, mem_scale=mem_scale,
                                     qk_scale=(MLA_NOPE + MLA_ROPE) ** -0.5 * LOG2E)
            qt, k2, vt, qm2 = _proj_call(
                kern, x2, seq, consts, _rope_tables(seq, MLA_ROPE),
                MLA_HEADS, MLA_QK, MLA_HEADS, tm_proj, tk)
        else:
            consts = [g_mix, gqa_w_in[j].astype(BF16),
                      gqa_q_norm[j].reshape(1, HEAD), gqa_k_norm[j].reshape(1, HEAD), mq]
            kern = functools.partial(_proj_gqa_kernel, mem_scale=mem_scale,
                                     qk_scale=HEAD ** -0.5 * LOG2E)
            qt, k2, vt, qm2 = _proj_call(
                kern, x2, seq, consts, _rope_tables(seq, HEAD),
                GQA_HEADS, HEAD, GQA_KV_HEADS, tm_proj, tk)
        mix = _attention(qt, k2.reshape(batch, seq, -1), vt, tq=2048)
        x2 = _post(x2, mix.reshape(n_rows, -1), qm2, k_mem[i], v_mem[i],
                   w_out[i].astype(BF16), seq, tm_post)
        x2 = _ffn(x2, norm_ffn[i], w_gate_up[i].astype(BF16), w_down[i].astype(BF16), tm_ffn)
    return x2.reshape(batch, seq, D_MODEL)
```

```python
import functools

import jax
import jax.numpy as jnp
from jax import lax
from jax.experimental import pallas as pl
from jax.experimental.pallas import tpu as pltpu

D_MODEL = 1024
GRID_W = 64
ROPE_THETA = 10000.0
EPS = 1e-6
LOG2E = 1.4426950408889634

MEM_HEADS = 4
HEAD = 128
MEM_W = MEM_HEADS * HEAD

MLA_HEADS = 8
MLA_Q_RANK = 384
MLA_KV_RANK = 256
MLA_NOPE = 128
MLA_ROPE = 64
MLA_QK = 2 * HEAD

GQA_HEADS = 8
GQA_KV_HEADS = 2

D_FF = 2816
FF_CHUNK = 256
COL_TILE = 256
EXP_ROWS = 64
SUM_ROWS = 16
UNROLL = 2

VMEM_LIMIT = 56 * 1024 * 1024

BF16 = jnp.bfloat16
F32 = jnp.float32


def _params(n_axes, flags=None):
    return pltpu.CompilerParams(dimension_semantics=("arbitrary",) * n_axes,
                                vmem_limit_bytes=VMEM_LIMIT, flags=flags)


def _rms(x, width=None):
    n = x.shape[-1] if width is None else width
    ms = jnp.sum(x * x, axis=-1, keepdims=True) * (1.0 / n)
    return x * lax.rsqrt(ms + EPS)


def _rope(x, cos, sin, half):
    lane = lax.broadcasted_iota(jnp.int32, x.shape, 1)
    upper = pltpu.roll(x, HEAD - half, axis=1)
    lower = pltpu.roll(x, half, axis=1)
    swapped = jnp.where((lane & (2 * half - 1)) < half, upper, lower)
    return x * cos + swapped * sin


def _rms_t(xt, width=None):
    n = xt.shape[0] if width is None else width
    ms = jnp.sum(xt * xt, axis=0, keepdims=True) * (1.0 / n)
    return xt * lax.rsqrt(ms + EPS)


def _rope_t(xt, cos_t, sin_t, half):
    blocks = [xt[r:r + half] for r in range(0, xt.shape[0], half)]
    swapped = jnp.concatenate([blocks[b ^ 1] for b in range(len(blocks))], axis=0)
    return xt * cos_t + swapped * sin_t


def _nt_dot(a, b):
    return lax.dot_general(a, b, (((1,), (1,)), ((), ())), preferred_element_type=F32)


def _memkv_kernel(mem_ref, g_ref, w_ref, kn_ref, k_ref, v_ref):
    mem_n = (_rms(mem_ref[...]) * g_ref[...]).astype(BF16)
    kv = jnp.dot(mem_n, w_ref[...], preferred_element_type=F32)
    for h in range(MEM_HEADS):
        kh = _rms(kv[:, h * HEAD:(h + 1) * HEAD]) * kn_ref[...]
        k_ref[:, h * HEAD:(h + 1) * HEAD] = kh.astype(BF16)
    v_ref[...] = kv[:, MEM_W:].astype(BF16)


def _memkv(mem, mem_norm, w_mem_kv, memk_norm):
    depth = w_mem_kv.shape[0]
    batch, n_mem, _ = mem.shape
    out = jax.ShapeDtypeStruct((depth, batch, n_mem, MEM_W), BF16)
    return pl.pallas_call(
        _memkv_kernel,
        grid=(depth, batch),
        in_specs=[
            pl.BlockSpec((None, n_mem, D_MODEL), lambda l, b: (b, 0, 0)),
            pl.BlockSpec((1, D_MODEL), lambda l, b: (0, 0)),
            pl.BlockSpec((None, D_MODEL, 2 * MEM_W), lambda l, b: (l, 0, 0)),
            pl.BlockSpec((None, 1, HEAD), lambda l, b: (l, 0, 0)),
        ],
        out_specs=[
            pl.BlockSpec((None, None, n_mem, MEM_W), lambda l, b: (l, b, 0, 0)),
            pl.BlockSpec((None, None, n_mem, MEM_W), lambda l, b: (l, b, 0, 0)),
        ],
        out_shape=[out, out],
        compiler_params=_params(2),
        name="mem_kv",
    )(mem, mem_norm.reshape(1, D_MODEL), w_mem_kv.astype(BF16),
      memk_norm.reshape(depth, 1, HEAD))


def _store_vt(vt_ref, head, v):
    tm = v.shape[0]
    vt_ref[head, :HEAD, :] = v.T.astype(BF16)
    row = lax.broadcasted_iota(jnp.int32, (SUM_ROWS, tm), 0)
    vt_ref[head, HEAD:, :] = jnp.where(row == 0, 1.0, 0.0).astype(BF16)


def _proj_mla_kernel(x_ref, g_ref, w_in_ref, qa_ref, w_qb_ref, kva_ref, w_kvb_ref,
                     qn_ref, qp_ref, kn_ref, kp_ref, mq_ref, cos_ref, sin_ref, cos_t_ref, sin_t_ref,
                     qt_ref, k_ref, vt_ref, qm_ref, *, qk_scale, mem_scale):
    h = (_rms(x_ref[...]) * g_ref[...]).astype(BF16)
    proj = jnp.dot(h, w_in_ref[...], preferred_element_type=F32)
    o_kv = MLA_Q_RANK
    o_pe = o_kv + MLA_KV_RANK
    o_qm = o_pe + HEAD
    c_q = (_rms(proj[:, :o_kv]) * qa_ref[...]).astype(BF16)
    c_kv = (_rms(proj[:, o_kv:o_pe]) * kva_ref[...]).astype(BF16)
    q = jnp.dot(c_q, w_qb_ref[...], preferred_element_type=F32)
    kv = jnp.dot(c_kv, w_kvb_ref[...], preferred_element_type=F32)
    half = MLA_ROPE // 4
    k_pe = _rope(_rms(proj[:, o_pe:o_qm], MLA_ROPE) * kp_ref[...],
                 cos_ref[...], sin_ref[...], half).astype(BF16)
    cos_t = cos_t_ref[...]
    sin_t = sin_t_ref[...]
    n_w = MLA_HEADS * HEAD
    for i in range(MLA_HEADS):
        lo = i * HEAD
        q_n = _rms_t(q[:, lo:lo + HEAD].T) * qn_ref[...] * qk_scale
        q_p = _rope_t(_rms_t(q[:, n_w + lo:n_w + lo + HEAD].T, MLA_ROPE) * qp_ref[...],
                      cos_t, sin_t, half)
        qt_ref[i, :HEAD, :] = q_n.astype(BF16)
        qt_ref[i, HEAD:, :] = (q_p * qk_scale).astype(BF16)
        k_n = _rms(kv[:, lo:lo + HEAD]) * kn_ref[...]
        k_ref[:, 2 * lo:2 * lo + HEAD] = k_n.astype(BF16)
        k_ref[:, 2 * lo + HEAD:2 * lo + 2 * HEAD] = k_pe
    for i in range(MLA_HEADS):
        lo = n_w + i * HEAD
        _store_vt(vt_ref, i, kv[:, lo:lo + HEAD])
    for i in range(MEM_HEADS):
        lo = i * HEAD
        q_m = _rms(proj[:, o_qm + lo:o_qm + lo + HEAD]) * mq_ref[...] * mem_scale
        qm_ref[:, lo:lo + HEAD] = q_m.astype(BF16)


def _proj_gqa_kernel(x_ref, g_ref, w_in_ref, qn_ref, kn_ref, mq_ref, cos_ref, sin_ref,
                     cos_t_ref, sin_t_ref, qt_ref, k_ref, vt_ref, qm_ref, *, qk_scale, mem_scale):
    h = (_rms(x_ref[...]) * g_ref[...]).astype(BF16)
    proj = jnp.dot(h, w_in_ref[...], preferred_element_type=F32)
    cos = cos_ref[...]
    sin = sin_ref[...]
    cos_t = cos_t_ref[...]
    sin_t = sin_t_ref[...]
    half = HEAD // 4
    o_k = GQA_HEADS * HEAD
    o_v = o_k + GQA_KV_HEADS * HEAD
    o_qm = o_v + GQA_KV_HEADS * HEAD
    for i in range(GQA_HEADS):
        lo = i * HEAD
        q_h = _rope_t(_rms_t(proj[:, lo:lo + HEAD].T) * qn_ref[...], cos_t, sin_t, half)
        qt_ref[i] = (q_h * qk_scale).astype(BF16)
    for i in range(GQA_KV_HEADS):
        lo = i * HEAD
        k_h = _rope(_rms(proj[:, o_k + lo:o_k + lo + HEAD]) * kn_ref[...], cos, sin, half)
        k_ref[:, lo:lo + HEAD] = k_h.astype(BF16)
    for i in range(GQA_KV_HEADS):
        lo = o_v + i * HEAD
        _store_vt(vt_ref, i, proj[:, lo:lo + HEAD])
    for i in range(MEM_HEADS):
        lo = i * HEAD
        q_m = _rms(proj[:, o_qm + lo:o_qm + lo + HEAD]) * mq_ref[...] * mem_scale
        qm_ref[:, lo:lo + HEAD] = q_m.astype(BF16)


def _row_tile(n_rows, seq, want):
    t = min(want, seq)
    assert seq % t == 0 and n_rows % t == 0
    return t


def _full(shape):
    return pl.BlockSpec(shape, lambda i: (0,) * len(shape))


def _proj_call(kernel, x2, seq, consts, tables, n_heads, dk, n_kv_heads, tm, tq, tk):
    n_rows = x2.shape[0]
    batch = n_rows // seq
    n_pos = seq // tm
    row = lambda w: pl.BlockSpec((tm, w), lambda i: (i, 0))
    tab = pl.BlockSpec((tm, HEAD), lambda i: (i % n_pos, 0))
    tab_t = pl.BlockSpec((HEAD, tm), lambda i: (0, i % n_pos))
    cos, sin = tables
    rows_out = lambda w: jax.ShapeDtypeStruct((n_rows, w), BF16)
    per_block = tq // tm
    qt_spec = pl.BlockSpec(
        (None, n_heads, None, dk, tm),
        lambda i: (i // n_pos, 0, (i % n_pos) // per_block, 0, (i % n_pos) % per_block))
    qt_out = jax.ShapeDtypeStruct((batch, n_heads, seq // tq, dk, tq), BF16)
    per_chunk = tk // tm
    vt_spec = pl.BlockSpec(
        (None, n_kv_heads, None, HEAD + SUM_ROWS, tm),
        lambda i: (i // n_pos, 0, (i % n_pos) // per_chunk, 0, (i % n_pos) % per_chunk))
    vt_out = jax.ShapeDtypeStruct((batch, n_kv_heads, seq // tk, HEAD + SUM_ROWS, tk), BF16)
    return pl.pallas_call(
        kernel,
        grid=(n_rows // tm,),
        in_specs=[row(D_MODEL)] + [_full(c.shape) for c in consts] + [tab, tab, tab_t, tab_t],
        out_specs=[qt_spec, row(n_kv_heads * dk), vt_spec, row(MEM_W)],
        out_shape=[qt_out, rows_out(n_kv_heads * dk), vt_out, rows_out(MEM_W)],
        compiler_params=_params(1),
        name=kernel.func.__name__.strip("_"),
    )(x2, *consts, cos, sin, cos.T, sin.T)


def _attn_kernel(qt_ref, k_ref, vt_ref, o_ref, s_ref, mx_ref, p_ref, a_ref, m_ref, acc_ref, *, tk):
    n_q, _, tq = qt_ref.shape
    n_chunks = vt_ref.shape[0]
    dv = o_ref.shape[1]
    total = n_q * n_chunks
    shift = n_chunks.bit_length() - 1
    col_tiles = [slice(c, c + COL_TILE) for c in range(0, tq, COL_TILE)]

    def scores(g, slot, cols):
        g = jnp.minimum(g, total - 1)
        start = pl.multiple_of((g & (n_chunks - 1)) * tk, tk)
        s = jnp.dot(k_ref[pl.ds(start, tk), :], qt_ref[g >> shift, :, cols],
                    preferred_element_type=F32)
        s_ref[slot, :, cols] = s
        mx_ref[slot, :, cols] = jnp.max(s.reshape(tk // 8, 8, COL_TILE), axis=0)

    def softmax(g, slot, cols):
        first_chunk = (g & (n_chunks - 1)) == 0
        m_old = jnp.where(first_chunk, -jnp.inf, m_ref[:, cols])
        m_new = jnp.maximum(m_old, jnp.max(mx_ref[slot, :, cols], axis=0, keepdims=True))
        m_ref[:, cols] = m_new
        a_ref[slot, :, cols] = jnp.exp2(m_old - m_new)
        for r in range(0, tk, EXP_ROWS):
            p = jnp.exp2(s_ref[slot, r:r + EXP_ROWS, cols] - m_new)
            p_ref[slot, r:r + EXP_ROWS, cols] = p.astype(BF16)

    def accumulate(g, slot, cols):
        g = jnp.maximum(g, 0)
        par = (g >> shift) & 1
        pv = jnp.dot(vt_ref[g & (n_chunks - 1)], p_ref[slot, :, cols],
                     preferred_element_type=F32)
        acc_ref[par, :, cols] = acc_ref[par, :, cols] * a_ref[slot, :, cols] + pv

    def finalize(qb):
        par = qb & 1
        rows = pl.ds(pl.multiple_of(qb * tq, tq), tq)
        out = acc_ref[par, :dv, :] / acc_ref[par, dv:dv + 1, :]
        o_ref[rows, :] = out.T.astype(o_ref.dtype)

    def half_step(g, cur):
        for cols in col_tiles:
            softmax(g, cur, cols)
            scores(g + 1, 1 - cur, cols)
            accumulate(g - 1, 1 - cur, cols)

    acc_ref[...] = jnp.zeros(acc_ref.shape, F32)
    p_ref[1] = jnp.zeros(p_ref.shape[1:], BF16)
    a_ref[1] = jnp.ones(a_ref.shape[1:], F32)
    for cols in col_tiles:
        scores(0, 0, cols)

    def step(t, carry):
        g = UNROLL * t
        for u in range(UNROLL):
            half_step(g + u, u & 1)

        @pl.when(jnp.logical_and(g > 0, (g & (n_chunks - 1)) == 0))
        def _():
            finalize((g >> shift) - 1)

        return carry

    lax.fori_loop(0, total // UNROLL, step, 0)
    for cols in col_tiles:
        accumulate(total - 1, 1, cols)
    finalize(n_q - 1)


def _attention(qt, k, vt):
    batch, n_heads, n_q, dk, tq = qt.shape
    _, n_kv_heads, n_chunks, dv_ext, tk = vt.shape
    seq = n_q * tq
    dv = dv_ext - SUM_ROWS
    group = n_heads // n_kv_heads
    assert n_chunks & (n_chunks - 1) == 0 and n_chunks % UNROLL == 0
    return pl.pallas_call(
        functools.partial(_attn_kernel, tk=tk),
        grid=(batch, n_heads),
        in_specs=[
            pl.BlockSpec((None, None, n_q, dk, tq), lambda b, h: (b, h, 0, 0, 0)),
            pl.BlockSpec((None, seq, dk), lambda b, h: (b, 0, h // group)),
            pl.BlockSpec((None, None, n_chunks, dv_ext, tk), lambda b, h: (b, h // group, 0, 0, 0)),
        ],
        out_specs=pl.BlockSpec((None, seq, dv), lambda b, h: (b, 0, h)),
        out_shape=jax.ShapeDtypeStruct((batch, seq, n_heads * dv), BF16),
        scratch_shapes=[
            pltpu.VMEM((2, tk, tq), F32),
            pltpu.VMEM((2, 8, tq), F32),
            pltpu.VMEM((2, tk, tq), BF16),
            pltpu.VMEM((2, 1, tq), F32),
            pltpu.VMEM((1, tq), F32),
            pltpu.VMEM((2, dv_ext, tq), F32),
        ],
        compiler_params=_params(2),
        name="attention",
    )(qt, k, vt)


def _post_kernel(x_ref, mix_ref, qm_ref, km_ref, vm_ref, w_ref, o_ref):
    n_mix = mix_ref.shape[-1]
    y = x_ref[...] + jnp.dot(mix_ref[...], w_ref[:n_mix, :], preferred_element_type=F32)
    o_mem = []
    for i in range(MEM_HEADS):
        lo = i * HEAD
        s = _nt_dot(qm_ref[:, lo:lo + HEAD], km_ref[:, lo:lo + HEAD])
        e = jnp.exp2(s - jnp.max(s, axis=-1, keepdims=True))
        pv = jnp.dot(e.astype(BF16), vm_ref[:, lo:lo + HEAD], preferred_element_type=F32)
        o_mem.append((pv / jnp.sum(e, axis=-1, keepdims=True)).astype(BF16))
    o_mem = jnp.concatenate(o_mem, axis=-1)
    o_ref[...] = y + jnp.dot(o_mem, w_ref[n_mix:, :], preferred_element_type=F32)


def _post(x2, mix2, qm2, k_mem, v_mem, w_out, seq, tm):
    n_rows = x2.shape[0]
    per_batch = seq // tm
    n_mem = k_mem.shape[1]
    row = lambda w: pl.BlockSpec((tm, w), lambda i: (i, 0))
    memspec = pl.BlockSpec((None, n_mem, MEM_W), lambda i: (i // per_batch, 0, 0))
    return pl.pallas_call(
        _post_kernel,
        grid=(n_rows // tm,),
        in_specs=[row(D_MODEL), row(mix2.shape[1]), row(MEM_W), memspec, memspec,
                  _full(w_out.shape)],
        out_specs=row(D_MODEL),
        out_shape=jax.ShapeDtypeStruct((n_rows, D_MODEL), F32),
        compiler_params=_params(1),
        name="mix_out",
    )(x2, mix2, qm2, k_mem, v_mem, w_out)


def _ffn_kernel(x_ref, g_ref, w_gu_ref, w_down_ref, o_ref):
    x = x_ref[...]
    h = (_rms(x) * g_ref[...]).astype(BF16)
    y = x
    for c in range(D_FF // FF_CHUNK):
        lo = c * FF_CHUNK
        gate = jnp.dot(h, w_gu_ref[:, lo:lo + FF_CHUNK], preferred_element_type=F32)
        up = jnp.dot(h, w_gu_ref[:, D_FF + lo:D_FF + lo + FF_CHUNK], preferred_element_type=F32)
        act = (gate * jax.nn.sigmoid(gate) * up).astype(BF16)
        y = y + jnp.dot(act, w_down_ref[lo:lo + FF_CHUNK, :], preferred_element_type=F32)
    o_ref[...] = y


def _ffn(x2, g, w_gu, w_down, tm):
    n_rows = x2.shape[0]
    row = pl.BlockSpec((tm, D_MODEL), lambda i: (i, 0))
    return pl.pallas_call(
        _ffn_kernel,
        grid=(n_rows // tm,),
        in_specs=[row, _full((1, D_MODEL)), _full(w_gu.shape), _full(w_down.shape)],
        out_specs=row,
        out_shape=jax.ShapeDtypeStruct((n_rows, D_MODEL), F32),
        compiler_params=_params(1),
        name="ffn",
    )(x2, g.reshape(1, D_MODEL), w_gu, w_down)


def _rope_tables(seq, dim):
    rows = seq // GRID_W
    row = jnp.repeat(jnp.arange(rows, dtype=F32), GRID_W)
    col = jnp.tile(jnp.arange(GRID_W, dtype=F32), rows)
    axis_dim = dim // 2
    inv = ROPE_THETA ** (-jnp.arange(0, axis_dim, 2, dtype=F32) / axis_dim)
    ang_r = row[:, None] * inv
    ang_c = col[:, None] * inv
    cos = jnp.concatenate([jnp.cos(ang_r)] * 2 + [jnp.cos(ang_c)] * 2, axis=-1)
    sin = jnp.concatenate([-jnp.sin(ang_r), jnp.sin(ang_r), -jnp.sin(ang_c), jnp.sin(ang_c)], axis=-1)
    pad = ((0, 0), (0, HEAD - dim))
    return jnp.pad(cos, pad), jnp.pad(sin, pad)


def _pad_lanes(v, width=HEAD):
    return jnp.pad(v, (0, width - v.shape[0])).reshape(1, width)


def _expand_rows(v, n_cols):
    return jnp.broadcast_to(jnp.pad(v, (0, HEAD - v.shape[0]))[:, None], (HEAD, n_cols))


def kernel(x, mem, mem_norm, norm_mix, norm_ffn, w_out, w_mem_kv, memq_norm, memk_norm, w_gate_up, w_down, mla_w_in, mla_q_a_norm, mla_w_q_b, mla_kv_a_norm, mla_w_kv_b, mla_q_norm, mla_k_norm, gqa_w_in, gqa_q_norm, gqa_k_norm):
    batch, seq, _ = x.shape
    depth = norm_mix.shape[0]
    n_rows = batch * seq
    tm_proj = _row_tile(n_rows, seq, 512)
    tk = _row_tile(n_rows, seq, 512)
    tq = _row_tile(n_rows, seq, 2048)
    tm_post = _row_tile(n_rows, seq, 512)
    tm_ffn = _row_tile(n_rows, seq, 512)
    mem_scale = HEAD ** -0.5 * LOG2E

    k_mem, v_mem = _memkv(mem, mem_norm, w_mem_kv, memk_norm)
    x2 = x.reshape(n_rows, D_MODEL)
    for i in range(depth):
        j = i // 2
        g_mix = norm_mix[i].reshape(1, D_MODEL)
        mq = memq_norm[i].reshape(1, HEAD)
        if i % 2 == 0:
            w_in = mla_w_in[j]
            o_qm = MLA_Q_RANK + MLA_KV_RANK + MLA_ROPE
            w_in = jnp.concatenate(
                [w_in[:, :o_qm], jnp.zeros((D_MODEL, HEAD - MLA_ROPE), F32), w_in[:, o_qm:]], axis=1)
            w_qb = mla_w_q_b[j].reshape(MLA_Q_RANK, MLA_HEADS, MLA_NOPE + MLA_ROPE)
            w_qb = jnp.concatenate(
                [w_qb[:, :, :MLA_NOPE].reshape(MLA_Q_RANK, -1),
                 jnp.pad(w_qb[:, :, MLA_NOPE:], ((0, 0), (0, 0), (0, HEAD - MLA_ROPE))
                         ).reshape(MLA_Q_RANK, -1)], axis=1)
            w_kvb = mla_w_kv_b[j].reshape(MLA_KV_RANK, MLA_HEADS, 2 * HEAD)
            w_kvb = jnp.concatenate([w_kvb[:, :, :HEAD].reshape(MLA_KV_RANK, -1),
                                     w_kvb[:, :, HEAD:].reshape(MLA_KV_RANK, -1)], axis=1)
            consts = [g_mix, w_in.astype(BF16),
                      mla_q_a_norm[j].reshape(1, -1), w_qb.astype(BF16),
                      mla_kv_a_norm[j].reshape(1, -1), w_kvb.astype(BF16),
                      _expand_rows(mla_q_norm[j][:MLA_NOPE], tm_proj),
                      _expand_rows(mla_q_norm[j][MLA_NOPE:], tm_proj),
                      _pad_lanes(mla_k_norm[j][:MLA_NOPE]), _pad_lanes(mla_k_norm[j][MLA_NOPE:]),
                      mq]
            kern = functools.partial(_proj_mla_kernel, mem_scale=mem_scale,
                                     qk_scale=(MLA_NOPE + MLA_ROPE) ** -0.5 * LOG2E)
            qt, k2, vt, qm2 = _proj_call(
                kern, x2, seq, consts, _rope_tables(seq, MLA_ROPE),
                MLA_HEADS, MLA_QK, MLA_HEADS, tm_proj, tq, tk)
        else:
            consts = [g_mix, gqa_w_in[j].astype(BF16),
                      _expand_rows(gqa_q_norm[j], tm_proj), gqa_k_norm[j].reshape(1, HEAD), mq]
            kern = functools.partial(_proj_gqa_kernel, mem_scale=mem_scale,
                                     qk_scale=HEAD ** -0.5 * LOG2E)
            qt, k2, vt, qm2 = _proj_call(
                kern, x2, seq, consts, _rope_tables(seq, HEAD),
                GQA_HEADS, HEAD, GQA_KV_HEADS, tm_proj, tq, tk)
        mix = _attention(qt, k2.reshape(batch, seq, -1), vt)
        x2 = _post(x2, mix.reshape(n_rows, -1), qm2, k_mem[i], v_mem[i],
                   w_out[i].astype(BF16), seq, tm_post)
        x2 = _ffn(x2, norm_ffn[i], w_gate_up[i].astype(BF16), w_down[i].astype(BF16), tm_ffn)
    return x2.reshape(batch, seq, D_MODEL)
```

```python
import functools

import jax
import jax.numpy as jnp
from jax import lax
from jax.experimental import pallas as pl
from jax.experimental.pallas import tpu as pltpu

D_MODEL = 1024
GRID_W = 64
ROPE_THETA = 10000.0
EPS = 1e-6
LOG2E = 1.4426950408889634

MEM_HEADS = 4
HEAD = 128
MEM_W = MEM_HEADS * HEAD

MLA_HEADS = 8
MLA_Q_RANK = 384
MLA_KV_RANK = 256
MLA_NOPE = 128
MLA_ROPE = 64
MLA_QK = 2 * HEAD

GQA_HEADS = 8
GQA_KV_HEADS = 2

D_FF = 2816
FF_CHUNK = 256
COL_TILE = 256
EXP_ROWS = 64
SUM_ROWS = 16
UNROLL = 2

VMEM_LIMIT = 56 * 1024 * 1024

BF16 = jnp.bfloat16
F32 = jnp.float32


def _params(n_axes, flags=None):
    return pltpu.CompilerParams(dimension_semantics=("arbitrary",) * n_axes,
                                vmem_limit_bytes=VMEM_LIMIT, flags=flags)


def _rms(x, width=None):
    n = x.shape[-1] if width is None else width
    ms = jnp.sum(x * x, axis=-1, keepdims=True) * (1.0 / n)
    return x * lax.rsqrt(ms + EPS)


def _rope(x, cos, sin, half):
    lane = lax.broadcasted_iota(jnp.int32, x.shape, 1)
    upper = pltpu.roll(x, HEAD - half, axis=1)
    lower = pltpu.roll(x, half, axis=1)
    swapped = jnp.where((lane & (2 * half - 1)) < half, upper, lower)
    return x * cos + swapped * sin


def _rms_t(xt, width=None):
    n = xt.shape[0] if width is None else width
    ms = jnp.sum(xt * xt, axis=0, keepdims=True) * (1.0 / n)
    return xt * lax.rsqrt(ms + EPS)


def _rope_t(xt, cos_t, sin_t, half):
    blocks = [xt[r:r + half] for r in range(0, xt.shape[0], half)]
    swapped = jnp.concatenate([blocks[b ^ 1] for b in range(len(blocks))], axis=0)
    return xt * cos_t + swapped * sin_t


def _nt_dot(a, b):
    return lax.dot_general(a, b, (((1,), (1,)), ((), ())), preferred_element_type=F32)


def _memkv_kernel(mem_ref, g_ref, w_ref, kn_ref, k_ref, v_ref):
    mem_n = (_rms(mem_ref[...]) * g_ref[...]).astype(BF16)
    kv = jnp.dot(mem_n, w_ref[...], preferred_element_type=F32)
    for h in range(MEM_HEADS):
        kh = _rms(kv[:, h * HEAD:(h + 1) * HEAD]) * kn_ref[...]
        k_ref[:, h * HEAD:(h + 1) * HEAD] = kh.astype(BF16)
    v_ref[...] = kv[:, MEM_W:].astype(BF16)


def _memkv(mem, mem_norm, w_mem_kv, memk_norm):
    depth = w_mem_kv.shape[0]
    batch, n_mem, _ = mem.shape
    out = jax.ShapeDtypeStruct((depth, batch, n_mem, MEM_W), BF16)
    return pl.pallas_call(
        _memkv_kernel,
        grid=(depth, batch),
        in_specs=[
            pl.BlockSpec((None, n_mem, D_MODEL), lambda l, b: (b, 0, 0)),
            pl.BlockSpec((1, D_MODEL), lambda l, b: (0, 0)),
            pl.BlockSpec((None, D_MODEL, 2 * MEM_W), lambda l, b: (l, 0, 0)),
            pl.BlockSpec((None, 1, HEAD), lambda l, b: (l, 0, 0)),
        ],
        out_specs=[
            pl.BlockSpec((None, None, n_mem, MEM_W), lambda l, b: (l, b, 0, 0)),
            pl.BlockSpec((None, None, n_mem, MEM_W), lambda l, b: (l, b, 0, 0)),
        ],
        out_shape=[out, out],
        compiler_params=_params(2),
        name="mem_kv",
    )(mem, mem_norm.reshape(1, D_MODEL), w_mem_kv.astype(BF16),
      memk_norm.reshape(depth, 1, HEAD))


def _store_vt(vt_ref, head, v):
    tm = v.shape[0]
    vt_ref[head, :HEAD, :] = v.T.astype(BF16)
    row = lax.broadcasted_iota(jnp.int32, (SUM_ROWS, tm), 0)
    vt_ref[head, HEAD:, :] = jnp.where(row == 0, 1.0, 0.0).astype(BF16)


def _proj_mla_kernel(x_ref, g_ref, w_in_ref, qa_ref, w_qb_ref, kva_ref, w_kvb_ref,
                     qn_ref, qp_ref, kn_ref, kp_ref, mq_ref, cos_ref, sin_ref, cos_t_ref, sin_t_ref,
                     qt_ref, k_ref, vt_ref, qm_ref, *, qk_scale, mem_scale):
    h = (_rms(x_ref[...]) * g_ref[...]).astype(BF16)
    proj = jnp.dot(h, w_in_ref[...], preferred_element_type=F32)
    o_kv = MLA_Q_RANK
    o_pe = o_kv + MLA_KV_RANK
    o_qm = o_pe + HEAD
    c_q = (_rms(proj[:, :o_kv]) * qa_ref[...]).astype(BF16)
    c_kv = (_rms(proj[:, o_kv:o_pe]) * kva_ref[...]).astype(BF16)
    q = jnp.dot(c_q, w_qb_ref[...], preferred_element_type=F32)
    kv = jnp.dot(c_kv, w_kvb_ref[...], preferred_element_type=F32)
    half = MLA_ROPE // 4
    k_pe = _rope(_rms(proj[:, o_pe:o_qm], MLA_ROPE) * kp_ref[...],
                 cos_ref[...], sin_ref[...], half).astype(BF16)
    cos_t = cos_t_ref[...]
    sin_t = sin_t_ref[...]
    n_w = MLA_HEADS * HEAD
    for i in range(MLA_HEADS):
        lo = i * HEAD
        q_n = _rms_t(q[:, lo:lo + HEAD].T) * qn_ref[...] * qk_scale
        q_p = _rope_t(_rms_t(q[:, n_w + lo:n_w + lo + HEAD].T, MLA_ROPE) * qp_ref[...],
                      cos_t, sin_t, half)
        qt_ref[i, :HEAD, :] = q_n.astype(BF16)
        qt_ref[i, HEAD:, :] = (q_p * qk_scale).astype(BF16)
        k_n = _rms(kv[:, lo:lo + HEAD]) * kn_ref[...]
        k_ref[:, 2 * lo:2 * lo + HEAD] = k_n.astype(BF16)
        k_ref[:, 2 * lo + HEAD:2 * lo + 2 * HEAD] = k_pe
    for i in range(MLA_HEADS):
        lo = n_w + i * HEAD
        _store_vt(vt_ref, i, kv[:, lo:lo + HEAD])
    for i in range(MEM_HEADS):
        lo = i * HEAD
        q_m = _rms(proj[:, o_qm + lo:o_qm + lo + HEAD]) * mq_ref[...] * mem_scale
        qm_ref[:, lo:lo + HEAD] = q_m.astype(BF16)


def _proj_gqa_kernel(x_ref, g_ref, w_in_ref, qn_ref, kn_ref, mq_ref, cos_ref, sin_ref,
                     cos_t_ref, sin_t_ref, qt_ref, k_ref, vt_ref, qm_ref, *, qk_scale, mem_scale):
    h = (_rms(x_ref[...]) * g_ref[...]).astype(BF16)
    proj = jnp.dot(h, w_in_ref[...], preferred_element_type=F32)
    cos = cos_ref[...]
    sin = sin_ref[...]
    cos_t = cos_t_ref[...]
    sin_t = sin_t_ref[...]
    half = HEAD // 4
    o_k = GQA_HEADS * HEAD
    o_v = o_k + GQA_KV_HEADS * HEAD
    o_qm = o_v + GQA_KV_HEADS * HEAD
    for i in range(GQA_HEADS):
        lo = i * HEAD
        q_h = _rope_t(_rms_t(proj[:, lo:lo + HEAD].T) * qn_ref[...], cos_t, sin_t, half)
        qt_ref[i] = (q_h * qk_scale).astype(BF16)
    for i in range(GQA_KV_HEADS):
        lo = i * HEAD
        k_h = _rope(_rms(proj[:, o_k + lo:o_k + lo + HEAD]) * kn_ref[...], cos, sin, half)
        k_ref[:, lo:lo + HEAD] = k_h.astype(BF16)
    for i in range(GQA_KV_HEADS):
        lo = o_v + i * HEAD
        _store_vt(vt_ref, i, proj[:, lo:lo + HEAD])
    for i in range(MEM_HEADS):
        lo = i * HEAD
        q_m = _rms(proj[:, o_qm + lo:o_qm + lo + HEAD]) * mq_ref[...] * mem_scale
        qm_ref[:, lo:lo + HEAD] = q_m.astype(BF16)


def _row_tile(n_rows, seq, want):
    t = min(want, seq)
    assert seq % t == 0 and n_rows % t == 0
    return t


def _full(shape):
    return pl.BlockSpec(shape, lambda i: (0,) * len(shape))


def _proj_call(kernel, x2, seq, consts, tables, n_heads, dk, n_kv_heads, tm, tq, tk):
    n_rows = x2.shape[0]
    batch = n_rows // seq
    n_pos = seq // tm
    row = lambda w: pl.BlockSpec((tm, w), lambda i: (i, 0))
    tab = pl.BlockSpec((tm, HEAD), lambda i: (i % n_pos, 0))
    tab_t = pl.BlockSpec((HEAD, tm), lambda i: (0, i % n_pos))
    cos, sin = tables
    rows_out = lambda w: jax.ShapeDtypeStruct((n_rows, w), BF16)
    per_block = tq // tm
    qt_spec = pl.BlockSpec(
        (None, n_heads, None, dk, tm),
        lambda i: (i // n_pos, 0, (i % n_pos) // per_block, 0, (i % n_pos) % per_block))
    qt_out = jax.ShapeDtypeStruct((batch, n_heads, seq // tq, dk, tq), BF16)
    per_chunk = tk // tm
    vt_spec = pl.BlockSpec(
        (None, n_kv_heads, None, HEAD + SUM_ROWS, tm),
        lambda i: (i // n_pos, 0, (i % n_pos) // per_chunk, 0, (i % n_pos) % per_chunk))
    vt_out = jax.ShapeDtypeStruct((batch, n_kv_heads, seq // tk, HEAD + SUM_ROWS, tk), BF16)
    return pl.pallas_call(
        kernel,
        grid=(n_rows // tm,),
        in_specs=[row(D_MODEL)] + [_full(c.shape) for c in consts] + [tab, tab, tab_t, tab_t],
        out_specs=[qt_spec, row(n_kv_heads * dk), vt_spec, row(MEM_W)],
        out_shape=[qt_out, rows_out(n_kv_heads * dk), vt_out, rows_out(MEM_W)],
        compiler_params=_params(1),
        name=kernel.func.__name__.strip("_"),
    )(x2, *consts, cos, sin, cos.T, sin.T)


def _attn_kernel(qt_ref, k_ref, vt_ref, o_ref, s_ref, mx_ref, p_ref, a_ref, m_ref, acc_ref, *, tk):
    n_q, _, tq = qt_ref.shape
    n_chunks = vt_ref.shape[0]
    dv = o_ref.shape[1]
    total = n_q * n_chunks
    shift = n_chunks.bit_length() - 1
    col_tiles = [slice(c, c + COL_TILE) for c in range(0, tq, COL_TILE)]

    def scores(g, slot, cols):
        g = jnp.minimum(g, total - 1)
        start = pl.multiple_of((g & (n_chunks - 1)) * tk, tk)
        s = jnp.dot(k_ref[pl.ds(start, tk), :], qt_ref[g >> shift, :, cols],
                    preferred_element_type=F32)
        s_ref[slot, :, cols] = s
        mx_ref[slot, :, cols] = jnp.max(s.reshape(tk // 8, 8, COL_TILE), axis=0)

    def softmax(g, slot, cols):
        first_chunk = (g & (n_chunks - 1)) == 0
        m_old = jnp.where(first_chunk, -jnp.inf, m_ref[:, cols])
        m_new = jnp.maximum(m_old, jnp.max(mx_ref[slot, :, cols], axis=0, keepdims=True))
        m_ref[:, cols] = m_new
        a_ref[slot, :, cols] = jnp.exp2(m_old - m_new)
        for r in range(0, tk, EXP_ROWS):
            p = jnp.exp2(s_ref[slot, r:r + EXP_ROWS, cols] - m_new)
            p_ref[slot, r:r + EXP_ROWS, cols] = p.astype(BF16)

    def accumulate(g, slot, cols):
        g = jnp.maximum(g, 0)
        par = (g >> shift) & 1
        pv = jnp.dot(vt_ref[g & (n_chunks - 1)], p_ref[slot, :, cols],
                     preferred_element_type=F32)
        acc_ref[par, :, cols] = acc_ref[par, :, cols] * a_ref[slot, :, cols] + pv

    def finalize(qb):
        par = qb & 1
        rows = pl.ds(pl.multiple_of(qb * tq, tq), tq)
        out = acc_ref[par, :dv, :] / acc_ref[par, dv:dv + 1, :]
        o_ref[rows, :] = out.T.astype(o_ref.dtype)

    def half_step(g, cur):
        for cols in col_tiles:
            softmax(g, cur, cols)
            scores(g + 1, 1 - cur, cols)
            accumulate(g - 1, 1 - cur, cols)

    acc_ref[...] = jnp.zeros(acc_ref.shape, F32)
    p_ref[1] = jnp.zeros(p_ref.shape[1:], BF16)
    a_ref[1] = jnp.ones(a_ref.shape[1:], F32)
    for cols in col_tiles:
        scores(0, 0, cols)

    def step(t, carry):
        g = UNROLL * t
        for u in range(UNROLL):
            half_step(g + u, u & 1)

        @pl.when(jnp.logical_and(g > 0, (g & (n_chunks - 1)) == 0))
        def _():
            finalize((g >> shift) - 1)

        return carry

    lax.fori_loop(0, total // UNROLL, step, 0)
    for cols in col_tiles:
        accumulate(total - 1, 1, cols)
    finalize(n_q - 1)


def _attention(qt, k, vt):
    batch, n_heads, n_q, dk, tq = qt.shape
    _, n_kv_heads, n_chunks, dv_ext, tk = vt.shape
    seq = n_q * tq
    dv = dv_ext - SUM_ROWS
    group = n_heads // n_kv_heads
    assert n_chunks & (n_chunks - 1) == 0 and n_chunks % UNROLL == 0
    return pl.pallas_call(
        functools.partial(_attn_kernel, tk=tk),
        grid=(batch, n_heads),
        in_specs=[
            pl.BlockSpec((None, None, n_q, dk, tq), lambda b, h: (b, h, 0, 0, 0)),
            pl.BlockSpec((None, seq, dk), lambda b, h: (b, 0, h // group)),
            pl.BlockSpec((None, None, n_chunks, dv_ext, tk), lambda b, h: (b, h // group, 0, 0, 0)),
        ],
        out_specs=pl.BlockSpec((None, seq, dv), lambda b, h: (b, 0, h)),
        out_shape=jax.ShapeDtypeStruct((batch, seq, n_heads * dv), BF16),
        scratch_shapes=[
            pltpu.VMEM((2, tk, tq), F32),
            pltpu.VMEM((2, 8, tq), F32),
            pltpu.VMEM((2, tk, tq), BF16),
            pltpu.VMEM((2, 1, tq), F32),
            pltpu.VMEM((1, tq), F32),
            pltpu.VMEM((2, dv_ext, tq), F32),
        ],
        compiler_params=_params(2),
        name="attention",
    )(qt, k, vt)


def _post_kernel(x_ref, mix_ref, qm_ref, km_ref, vm_ref, w_ref, o_ref):
    n_mix = mix_ref.shape[-1]
    y = x_ref[...] + jnp.dot(mix_ref[...], w_ref[:n_mix, :], preferred_element_type=F32)
    o_mem = []
    for i in range(MEM_HEADS):
        lo = i * HEAD
        s = _nt_dot(qm_ref[:, lo:lo + HEAD], km_ref[:, lo:lo + HEAD])
        e = jnp.exp2(s - jnp.max(s, axis=-1, keepdims=True))
        pv = jnp.dot(e.astype(BF16), vm_ref[:, lo:lo + HEAD], preferred_element_type=F32)
        o_mem.append((pv / jnp.sum(e, axis=-1, keepdims=True)).astype(BF16))
    o_mem = jnp.concatenate(o_mem, axis=-1)
    o_ref[...] = y + jnp.dot(o_mem, w_ref[n_mix:, :], preferred_element_type=F32)


def _post(x2, mix2, qm2, k_mem, v_mem, w_out, seq, tm):
    n_rows = x2.shape[0]
    per_batch = seq // tm
    n_mem = k_mem.shape[1]
    row = lambda w: pl.BlockSpec((tm, w), lambda i: (i, 0))
    memspec = pl.BlockSpec((None, n_mem, MEM_W), lambda i: (i // per_batch, 0, 0))
    return pl.pallas_call(
        _post_kernel,
        grid=(n_rows // tm,),
        in_specs=[row(D_MODEL), row(mix2.shape[1]), row(MEM_W), memspec, memspec,
                  _full(w_out.shape)],
        out_specs=row(D_MODEL),
        out_shape=jax.ShapeDtypeStruct((n_rows, D_MODEL), F32),
        compiler_params=_params(1),
        name="mix_out",
    )(x2, mix2, qm2, k_mem, v_mem, w_out)


def _ffn_kernel(x_ref, g_ref, w_gu_ref, w_down_ref, o_ref):
    x = x_ref[...]
    h = (_rms(x) * g_ref[...]).astype(BF16)
    y = x
    for c in range(D_FF // FF_CHUNK):
        lo = c * FF_CHUNK
        gate = jnp.dot(h, w_gu_ref[:, lo:lo + FF_CHUNK], preferred_element_type=F32)
        up = jnp.dot(h, w_gu_ref[:, D_FF + lo:D_FF + lo + FF_CHUNK], preferred_element_type=F32)
        act = (gate * jax.nn.sigmoid(gate) * up).astype(BF16)
        y = y + jnp.dot(act, w_down_ref[lo:lo + FF_CHUNK, :], preferred_element_type=F32)
    o_ref[...] = y


def _ffn(x2, g, w_gu, w_down, tm):
    n_rows = x2.shape[0]
    row = pl.BlockSpec((tm, D_MODEL), lambda i: (i, 0))
    return pl.pallas_call(
        _ffn_kernel,
        grid=(n_rows // tm,),
        in_specs=[row, _full((1, D_MODEL)), _full(w_gu.shape), _full(w_down.shape)],
        out_specs=row,
        out_shape=jax.ShapeDtypeStruct((n_rows, D_MODEL), F32),
        compiler_params=_params(1),
        name="ffn",
    )(x2, g.reshape(1, D_MODEL), w_gu, w_down)


def _rope_tables(seq, dim):
    rows = seq // GRID_W
    row = jnp.repeat(jnp.arange(rows, dtype=F32), GRID_W)
    col = jnp.tile(jnp.arange(GRID_W, dtype=F32), rows)
    axis_dim = dim // 2
    inv = ROPE_THETA ** (-jnp.arange(0, axis_dim, 2, dtype=F32) / axis_dim)
    ang_r = row[:, None] * inv
    ang_c = col[:, None] * inv
    cos = jnp.concatenate([jnp.cos(ang_r)] * 2 + [jnp.cos(ang_c)] * 2, axis=-1)
    sin = jnp.concatenate([-jnp.sin(ang_r), jnp.sin(ang_r), -jnp.sin(ang_c), jnp.sin(ang_c)], axis=-1)
    pad = ((0, 0), (0, HEAD - dim))
    return jnp.pad(cos, pad), jnp.pad(sin, pad)


def _pad_lanes(v, width=HEAD):
    return jnp.pad(v, (0, width - v.shape[0])).reshape(1, width)


def _expand_rows(v, n_cols):
    return jnp.broadcast_to(jnp.pad(v, (0, HEAD - v.shape[0]))[:, None], (HEAD, n_cols))


def kernel(x, mem, mem_norm, norm_mix, norm_ffn, w_out, w_mem_kv, memq_norm, memk_norm, w_gate_up, w_down, mla_w_in, mla_q_a_norm, mla_w_q_b, mla_kv_a_norm, mla_w_kv_b, mla_q_norm, mla_k_norm, gqa_w_in, gqa_q_norm, gqa_k_norm):
    batch, seq, _ = x.shape
    depth = norm_mix.shape[0]
    n_rows = batch * seq
    tm_proj = _row_tile(n_rows, seq, 512)
    tk = _row_tile(n_rows, seq, 1024)
    tq = _row_tile(n_rows, seq, 2048)
    tm_post = _row_tile(n_rows, seq, 512)
    tm_ffn = _row_tile(n_rows, seq, 512)
    mem_scale = HEAD ** -0.5 * LOG2E

    k_mem, v_mem = _memkv(mem, mem_norm, w_mem_kv, memk_norm)
    x2 = x.reshape(n_rows, D_MODEL)
    for i in range(depth):
        j = i // 2
        g_mix = norm_mix[i].reshape(1, D_MODEL)
        mq = memq_norm[i].reshape(1, HEAD)
        if i % 2 == 0:
            w_in = mla_w_in[j]
            o_qm = MLA_Q_RANK + MLA_KV_RANK + MLA_ROPE
            w_in = jnp.concatenate(
                [w_in[:, :o_qm], jnp.zeros((D_MODEL, HEAD - MLA_ROPE), F32), w_in[:, o_qm:]], axis=1)
            w_qb = mla_w_q_b[j].reshape(MLA_Q_RANK, MLA_HEADS, MLA_NOPE + MLA_ROPE)
            w_qb = jnp.concatenate(
                [w_qb[:, :, :MLA_NOPE].reshape(MLA_Q_RANK, -1),
                 jnp.pad(w_qb[:, :, MLA_NOPE:], ((0, 0), (0, 0), (0, HEAD - MLA_ROPE))
                         ).reshape(MLA_Q_RANK, -1)], axis=1)
            w_kvb = mla_w_kv_b[j].reshape(MLA_KV_RANK, MLA_HEADS, 2 * HEAD)
            w_kvb = jnp.concatenate([w_kvb[:, :, :HEAD].reshape(MLA_KV_RANK, -1),
                                     w_kvb[:, :, HEAD:].reshape(MLA_KV_RANK, -1)], axis=1)
            consts = [g_mix, w_in.astype(BF16),
                      mla_q_a_norm[j].reshape(1, -1), w_qb.astype(BF16),
                      mla_kv_a_norm[j].reshape(1, -1), w_kvb.astype(BF16),
                      _expand_rows(mla_q_norm[j][:MLA_NOPE], tm_proj),
                      _expand_rows(mla_q_norm[j][MLA_NOPE:], tm_proj),
                      _pad_lanes(mla_k_norm[j][:MLA_NOPE]), _pad_lanes(mla_k_norm[j][MLA_NOPE:]),
                      mq]
            kern = functools.partial(_proj_mla_kernel, mem_scale=mem_scale,
                                     qk_scale=(MLA_NOPE + MLA_ROPE) ** -0.5 * LOG2E)
            qt, k2, vt, qm2 = _proj_call(
                kern, x2, seq, consts, _rope_tables(seq, MLA_ROPE),
                MLA_HEADS, MLA_QK, MLA_HEADS, tm_proj, tq, tk)
        else:
            consts = [g_mix, gqa_w_in[j].astype(BF16),
                      _expand_rows(gqa_q_norm[j], tm_proj), gqa_k_norm[j].reshape(1, HEAD), mq]
            kern = functools.partial(_proj_gqa_kernel, mem_scale=mem_scale,
                                     qk_scale=HEAD ** -0.5 * LOG2E)
            qt, k2, vt, qm2 = _proj_call(
                kern, x2, seq, consts, _rope_tables(seq, HEAD),
                GQA_HEADS, HEAD, GQA_KV_HEADS, tm_proj, tq, tk)
        mix = _attention(qt, k2.reshape(batch, seq, -1), vt)
        x2 = _post(x2, mix.reshape(n_rows, -1), qm2, k_mem[i], v_mem[i],
                   w_out[i].astype(BF16), seq, tm_post)
        x2 = _ffn(x2, norm_ffn[i], w_gate_up[i].astype(BF16), w_down[i].astype(BF16), tm_ffn)
    return x2.reshape(batch, seq, D_MODEL)
```

```python
import functools

import jax
import jax.numpy as jnp
from jax import lax
from jax.experimental import pallas as pl
from jax.experimental.pallas import tpu as pltpu

D_MODEL = 1024
GRID_W = 64
ROPE_THETA = 10000.0
EPS = 1e-6
LOG2E = 1.4426950408889634

MEM_HEADS = 4
HEAD = 128
MEM_W = MEM_HEADS * HEAD

MLA_HEADS = 8
MLA_Q_RANK = 384
MLA_KV_RANK = 256
MLA_NOPE = 128
MLA_ROPE = 64
MLA_QK = 2 * HEAD

GQA_HEADS = 8
GQA_KV_HEADS = 2

D_FF = 2816
FF_CHUNK = 256
COL_TILE = 256
EXP_ROWS = 64
SUM_ROWS = 16
UNROLL = 4

VMEM_LIMIT = 56 * 1024 * 1024

BF16 = jnp.bfloat16
F32 = jnp.float32


def _params(n_axes, flags=None):
    return pltpu.CompilerParams(dimension_semantics=("arbitrary",) * n_axes,
                                vmem_limit_bytes=VMEM_LIMIT, flags=flags)


def _rms(x, width=None):
    n = x.shape[-1] if width is None else width
    ms = jnp.sum(x * x, axis=-1, keepdims=True) * (1.0 / n)
    return x * lax.rsqrt(ms + EPS)


def _rope(x, cos, sin, half):
    lane = lax.broadcasted_iota(jnp.int32, x.shape, 1)
    upper = pltpu.roll(x, HEAD - half, axis=1)
    lower = pltpu.roll(x, half, axis=1)
    swapped = jnp.where((lane & (2 * half - 1)) < half, upper, lower)
    return x * cos + swapped * sin


def _rms_t(xt, width=None):
    n = xt.shape[0] if width is None else width
    ms = jnp.sum(xt * xt, axis=0, keepdims=True) * (1.0 / n)
    return xt * lax.rsqrt(ms + EPS)


def _rope_t(xt, cos_t, sin_t, half):
    blocks = [xt[r:r + half] for r in range(0, xt.shape[0], half)]
    swapped = jnp.concatenate([blocks[b ^ 1] for b in range(len(blocks))], axis=0)
    return xt * cos_t + swapped * sin_t


def _nt_dot(a, b):
    return lax.dot_general(a, b, (((1,), (1,)), ((), ())), preferred_element_type=F32)


def _memkv_kernel(mem_ref, g_ref, w_ref, kn_ref, k_ref, v_ref):
    mem_n = (_rms(mem_ref[...]) * g_ref[...]).astype(BF16)
    kv = jnp.dot(mem_n, w_ref[...], preferred_element_type=F32)
    for h in range(MEM_HEADS):
        kh = _rms(kv[:, h * HEAD:(h + 1) * HEAD]) * kn_ref[...]
        k_ref[:, h * HEAD:(h + 1) * HEAD] = kh.astype(BF16)
    v_ref[...] = kv[:, MEM_W:].astype(BF16)


def _memkv(mem, mem_norm, w_mem_kv, memk_norm):
    depth = w_mem_kv.shape[0]
    batch, n_mem, _ = mem.shape
    out = jax.ShapeDtypeStruct((depth, batch, n_mem, MEM_W), BF16)
    return pl.pallas_call(
        _memkv_kernel,
        grid=(depth, batch),
        in_specs=[
            pl.BlockSpec((None, n_mem, D_MODEL), lambda l, b: (b, 0, 0)),
            pl.BlockSpec((1, D_MODEL), lambda l, b: (0, 0)),
            pl.BlockSpec((None, D_MODEL, 2 * MEM_W), lambda l, b: (l, 0, 0)),
            pl.BlockSpec((None, 1, HEAD), lambda l, b: (l, 0, 0)),
        ],
        out_specs=[
            pl.BlockSpec((None, None, n_mem, MEM_W), lambda l, b: (l, b, 0, 0)),
            pl.BlockSpec((None, None, n_mem, MEM_W), lambda l, b: (l, b, 0, 0)),
        ],
        out_shape=[out, out],
        compiler_params=_params(2),
        name="mem_kv",
    )(mem, mem_norm.reshape(1, D_MODEL), w_mem_kv.astype(BF16),
      memk_norm.reshape(depth, 1, HEAD))


def _store_vt(vt_ref, head, v):
    tm = v.shape[0]
    vt_ref[head, :HEAD, :] = v.T.astype(BF16)
    row = lax.broadcasted_iota(jnp.int32, (SUM_ROWS, tm), 0)
    vt_ref[head, HEAD:, :] = jnp.where(row == 0, 1.0, 0.0).astype(BF16)


def _proj_mla_kernel(x_ref, g_ref, w_in_ref, qa_ref, w_qb_ref, kva_ref, w_kvb_ref,
                     qn_ref, qp_ref, kn_ref, kp_ref, mq_ref, cos_ref, sin_ref, cos_t_ref, sin_t_ref,
                     qt_ref, k_ref, vt_ref, qm_ref, *, qk_scale, mem_scale):
    h = (_rms(x_ref[...]) * g_ref[...]).astype(BF16)
    proj = jnp.dot(h, w_in_ref[...], preferred_element_type=F32)
    o_kv = MLA_Q_RANK
    o_pe = o_kv + MLA_KV_RANK
    o_qm = o_pe + HEAD
    c_q = (_rms(proj[:, :o_kv]) * qa_ref[...]).astype(BF16)
    c_kv = (_rms(proj[:, o_kv:o_pe]) * kva_ref[...]).astype(BF16)
    q = jnp.dot(c_q, w_qb_ref[...], preferred_element_type=F32)
    kv = jnp.dot(c_kv, w_kvb_ref[...], preferred_element_type=F32)
    half = MLA_ROPE // 4
    k_pe = _rope(_rms(proj[:, o_pe:o_qm], MLA_ROPE) * kp_ref[...],
                 cos_ref[...], sin_ref[...], half).astype(BF16)
    cos_t = cos_t_ref[...]
    sin_t = sin_t_ref[...]
    n_w = MLA_HEADS * HEAD
    for i in range(MLA_HEADS):
        lo = i * HEAD
        q_n = _rms_t(q[:, lo:lo + HEAD].T) * qn_ref[...] * qk_scale
        q_p = _rope_t(_rms_t(q[:, n_w + lo:n_w + lo + HEAD].T, MLA_ROPE) * qp_ref[...],
                      cos_t, sin_t, half)
        qt_ref[i, :HEAD, :] = q_n.astype(BF16)
        qt_ref[i, HEAD:, :] = (q_p * qk_scale).astype(BF16)
        k_n = _rms(kv[:, lo:lo + HEAD]) * kn_ref[...]
        k_ref[:, 2 * lo:2 * lo + HEAD] = k_n.astype(BF16)
        k_ref[:, 2 * lo + HEAD:2 * lo + 2 * HEAD] = k_pe
    for i in range(MLA_HEADS):
        lo = n_w + i * HEAD
        _store_vt(vt_ref, i, kv[:, lo:lo + HEAD])
    for i in range(MEM_HEADS):
        lo = i * HEAD
        q_m = _rms(proj[:, o_qm + lo:o_qm + lo + HEAD]) * mq_ref[...] * mem_scale
        qm_ref[:, lo:lo + HEAD] = q_m.astype(BF16)


def _proj_gqa_kernel(x_ref, g_ref, w_in_ref, qn_ref, kn_ref, mq_ref, cos_ref, sin_ref,
                     cos_t_ref, sin_t_ref, qt_ref, k_ref, vt_ref, qm_ref, *, qk_scale, mem_scale):
    h = (_rms(x_ref[...]) * g_ref[...]).astype(BF16)
    proj = jnp.dot(h, w_in_ref[...], preferred_element_type=F32)
    cos = cos_ref[...]
    sin = sin_ref[...]
    cos_t = cos_t_ref[...]
    sin_t = sin_t_ref[...]
    half = HEAD // 4
    o_k = GQA_HEADS * HEAD
    o_v = o_k + GQA_KV_HEADS * HEAD
    o_qm = o_v + GQA_KV_HEADS * HEAD
    for i in range(GQA_HEADS):
        lo = i * HEAD
        q_h = _rope_t(_rms_t(proj[:, lo:lo + HEAD].T) * qn_ref[...], cos_t, sin_t, half)
        qt_ref[i] = (q_h * qk_scale).astype(BF16)
    for i in range(GQA_KV_HEADS):
        lo = i * HEAD
        k_h = _rope(_rms(proj[:, o_k + lo:o_k + lo + HEAD]) * kn_ref[...], cos, sin, half)
        k_ref[:, lo:lo + HEAD] = k_h.astype(BF16)
    for i in range(GQA_KV_HEADS):
        lo = o_v + i * HEAD
        _store_vt(vt_ref, i, proj[:, lo:lo + HEAD])
    for i in range(MEM_HEADS):
        lo = i * HEAD
        q_m = _rms(proj[:, o_qm + lo:o_qm + lo + HEAD]) * mq_ref[...] * mem_scale
        qm_ref[:, lo:lo + HEAD] = q_m.astype(BF16)


def _row_tile(n_rows, seq, want):
    t = min(want, seq)
    assert seq % t == 0 and n_rows % t == 0
    return t


def _full(shape):
    return pl.BlockSpec(shape, lambda i: (0,) * len(shape))


def _proj_call(kernel, x2, seq, consts, tables, n_heads, dk, n_kv_heads, tm, tq, tk):
    n_rows = x2.shape[0]
    batch = n_rows // seq
    n_pos = seq // tm
    row = lambda w: pl.BlockSpec((tm, w), lambda i: (i, 0))
    tab = pl.BlockSpec((tm, HEAD), lambda i: (i % n_pos, 0))
    tab_t = pl.BlockSpec((HEAD, tm), lambda i: (0, i % n_pos))
    cos, sin = tables
    rows_out = lambda w: jax.ShapeDtypeStruct((n_rows, w), BF16)
    per_block = tq // tm
    qt_spec = pl.BlockSpec(
        (None, n_heads, None, dk, tm),
        lambda i: (i // n_pos, 0, (i % n_pos) // per_block, 0, (i % n_pos) % per_block))
    qt_out = jax.ShapeDtypeStruct((batch, n_heads, seq // tq, dk, tq), BF16)
    per_chunk = tk // tm
    vt_spec = pl.BlockSpec(
        (None, n_kv_heads, None, HEAD + SUM_ROWS, tm),
        lambda i: (i // n_pos, 0, (i % n_pos) // per_chunk, 0, (i % n_pos) % per_chunk))
    vt_out = jax.ShapeDtypeStruct((batch, n_kv_heads, seq // tk, HEAD + SUM_ROWS, tk), BF16)
    return pl.pallas_call(
        kernel,
        grid=(n_rows // tm,),
        in_specs=[row(D_MODEL)] + [_full(c.shape) for c in consts] + [tab, tab, tab_t, tab_t],
        out_specs=[qt_spec, row(n_kv_heads * dk), vt_spec, row(MEM_W)],
        out_shape=[qt_out, rows_out(n_kv_heads * dk), vt_out, rows_out(MEM_W)],
        compiler_params=_params(1),
        name=kernel.func.__name__.strip("_"),
    )(x2, *consts, cos, sin, cos.T, sin.T)


def _attn_kernel(qt_ref, k_ref, vt_ref, o_ref, s_ref, mx_ref, p_ref, a_ref, m_ref, acc_ref, *, tk):
    n_q, _, tq = qt_ref.shape
    n_chunks = vt_ref.shape[0]
    dv = o_ref.shape[1]
    total = n_q * n_chunks
    shift = n_chunks.bit_length() - 1
    col_tiles = [slice(c, c + COL_TILE) for c in range(0, tq, COL_TILE)]

    def scores(g, slot, cols):
        g = jnp.minimum(g, total - 1)
        start = pl.multiple_of((g & (n_chunks - 1)) * tk, tk)
        s = jnp.dot(k_ref[pl.ds(start, tk), :], qt_ref[g >> shift, :, cols],
                    preferred_element_type=F32)
        s_ref[slot, :, cols] = s
        mx_ref[slot, :, cols] = jnp.max(s.reshape(tk // 8, 8, COL_TILE), axis=0)

    def softmax(g, slot, p_slot, cols):
        first_chunk = (g & (n_chunks - 1)) == 0
        m_old = jnp.where(first_chunk, -jnp.inf, m_ref[:, cols])
        m_new = jnp.maximum(m_old, jnp.max(mx_ref[slot, :, cols], axis=0, keepdims=True))
        m_ref[:, cols] = m_new
        a_ref[p_slot, :, cols] = jnp.exp2(m_old - m_new)
        for r in range(0, tk, EXP_ROWS):
            p = jnp.exp2(s_ref[slot, r:r + EXP_ROWS, cols] - m_new)
            p_ref[p_slot, r:r + EXP_ROWS, cols] = p.astype(BF16)

    def accumulate(g, slot, cols):
        g = jnp.maximum(g, 0)
        par = (g >> shift) & 1
        pv = jnp.dot(vt_ref[g & (n_chunks - 1)], p_ref[slot, :, cols],
                     preferred_element_type=F32)
        acc_ref[par, :, cols] = acc_ref[par, :, cols] * a_ref[slot, :, cols] + pv

    def finalize(qb):
        par = qb & 1
        rows = pl.ds(pl.multiple_of(qb * tq, tq), tq)
        out = acc_ref[par, :dv, :] / acc_ref[par, dv:dv + 1, :]
        o_ref[rows, :] = out.T.astype(o_ref.dtype)

    def half_step(g, u):
        for cols in col_tiles:
            softmax(g, u % 2, u % 4, cols)
            scores(g + 1, (u + 1) % 2, cols)
            accumulate(g - 2, (u + 2) % 4, cols)

    acc_ref[...] = jnp.zeros(acc_ref.shape, F32)
    p_ref[2:] = jnp.zeros((2,) + p_ref.shape[1:], BF16)
    a_ref[2:] = jnp.ones((2,) + a_ref.shape[1:], F32)
    for cols in col_tiles:
        scores(0, 0, cols)

    def step(t, carry):
        g = UNROLL * t
        for u in range(UNROLL):
            half_step(g + u, u)

        @pl.when(jnp.logical_and(g > 0, (g & (n_chunks - 1)) == 0))
        def _():
            finalize((g >> shift) - 1)

        return carry

    lax.fori_loop(0, total // UNROLL, step, 0)
    for cols in col_tiles:
        accumulate(total - 2, 2, cols)
        accumulate(total - 1, 3, cols)
    finalize(n_q - 1)


def _attention(qt, k, vt):
    batch, n_heads, n_q, dk, tq = qt.shape
    _, n_kv_heads, n_chunks, dv_ext, tk = vt.shape
    seq = n_q * tq
    dv = dv_ext - SUM_ROWS
    group = n_heads // n_kv_heads
    assert n_chunks & (n_chunks - 1) == 0 and n_chunks % UNROLL == 0
    return pl.pallas_call(
        functools.partial(_attn_kernel, tk=tk),
        grid=(batch, n_heads),
        in_specs=[
            pl.BlockSpec((None, None, n_q, dk, tq), lambda b, h: (b, h, 0, 0, 0)),
            pl.BlockSpec((None, seq, dk), lambda b, h: (b, 0, h // group)),
            pl.BlockSpec((None, None, n_chunks, dv_ext, tk), lambda b, h: (b, h // group, 0, 0, 0)),
        ],
        out_specs=pl.BlockSpec((None, seq, dv), lambda b, h: (b, 0, h)),
        out_shape=jax.ShapeDtypeStruct((batch, seq, n_heads * dv), BF16),
        scratch_shapes=[
            pltpu.VMEM((2, tk, tq), F32),
            pltpu.VMEM((2, 8, tq), F32),
            pltpu.VMEM((4, tk, tq), BF16),
            pltpu.VMEM((4, 1, tq), F32),
            pltpu.VMEM((1, tq), F32),
            pltpu.VMEM((2, dv_ext, tq), F32),
        ],
        compiler_params=_params(2),
        name="attention",
    )(qt, k, vt)


def _post_kernel(x_ref, mix_ref, qm_ref, km_ref, vm_ref, w_ref, o_ref):
    n_mix = mix_ref.shape[-1]
    y = x_ref[...] + jnp.dot(mix_ref[...], w_ref[:n_mix, :], preferred_element_type=F32)
    o_mem = []
    for i in range(MEM_HEADS):
        lo = i * HEAD
        s = _nt_dot(qm_ref[:, lo:lo + HEAD], km_ref[:, lo:lo + HEAD])
        e = jnp.exp2(s - jnp.max(s, axis=-1, keepdims=True))
        pv = jnp.dot(e.astype(BF16), vm_ref[:, lo:lo + HEAD], preferred_element_type=F32)
        o_mem.append((pv / jnp.sum(e, axis=-1, keepdims=True)).astype(BF16))
    o_mem = jnp.concatenate(o_mem, axis=-1)
    o_ref[...] = y + jnp.dot(o_mem, w_ref[n_mix:, :], preferred_element_type=F32)


def _post(x2, mix2, qm2, k_mem, v_mem, w_out, seq, tm):
    n_rows = x2.shape[0]
    per_batch = seq // tm
    n_mem = k_mem.shape[1]
    row = lambda w: pl.BlockSpec((tm, w), lambda i: (i, 0))
    memspec = pl.BlockSpec((None, n_mem, MEM_W), lambda i: (i // per_batch, 0, 0))
    return pl.pallas_call(
        _post_kernel,
        grid=(n_rows // tm,),
        in_specs=[row(D_MODEL), row(mix2.shape[1]), row(MEM_W), memspec, memspec,
                  _full(w_out.shape)],
        out_specs=row(D_MODEL),
        out_shape=jax.ShapeDtypeStruct((n_rows, D_MODEL), F32),
        compiler_params=_params(1),
        name="mix_out",
    )(x2, mix2, qm2, k_mem, v_mem, w_out)


def _ffn_kernel(x_ref, g_ref, w_gu_ref, w_down_ref, o_ref):
    x = x_ref[...]
    h = (_rms(x) * g_ref[...]).astype(BF16)
    y = x
    for c in range(D_FF // FF_CHUNK):
        lo = c * FF_CHUNK
        gate = jnp.dot(h, w_gu_ref[:, lo:lo + FF_CHUNK], preferred_element_type=F32)
        up = jnp.dot(h, w_gu_ref[:, D_FF + lo:D_FF + lo + FF_CHUNK], preferred_element_type=F32)
        act = (gate * jax.nn.sigmoid(gate) * up).astype(BF16)
        y = y + jnp.dot(act, w_down_ref[lo:lo + FF_CHUNK, :], preferred_element_type=F32)
    o_ref[...] = y


def _ffn(x2, g, w_gu, w_down, tm):
    n_rows = x2.shape[0]
    row = pl.BlockSpec((tm, D_MODEL), lambda i: (i, 0))
    return pl.pallas_call(
        _ffn_kernel,
        grid=(n_rows // tm,),
        in_specs=[row, _full((1, D_MODEL)), _full(w_gu.shape), _full(w_down.shape)],
        out_specs=row,
        out_shape=jax.ShapeDtypeStruct((n_rows, D_MODEL), F32),
        compiler_params=_params(1),
        name="ffn",
    )(x2, g.reshape(1, D_MODEL), w_gu, w_down)


def _rope_tables(seq, dim):
    rows = seq // GRID_W
    row = jnp.repeat(jnp.arange(rows, dtype=F32), GRID_W)
    col = jnp.tile(jnp.arange(GRID_W, dtype=F32), rows)
    axis_dim = dim // 2
    inv = ROPE_THETA ** (-jnp.arange(0, axis_dim, 2, dtype=F32) / axis_dim)
    ang_r = row[:, None] * inv
    ang_c = col[:, None] * inv
    cos = jnp.concatenate([jnp.cos(ang_r)] * 2 + [jnp.cos(ang_c)] * 2, axis=-1)
    sin = jnp.concatenate([-jnp.sin(ang_r), jnp.sin(ang_r), -jnp.sin(ang_c), jnp.sin(ang_c)], axis=-1)
    pad = ((0, 0), (0, HEAD - dim))
    return jnp.pad(cos, pad), jnp.pad(sin, pad)


def _pad_lanes(v, width=HEAD):
    return jnp.pad(v, (0, width - v.shape[0])).reshape(1, width)


def _expand_rows(v, n_cols):
    return jnp.broadcast_to(jnp.pad(v, (0, HEAD - v.shape[0]))[:, None], (HEAD, n_cols))


def kernel(x, mem, mem_norm, norm_mix, norm_ffn, w_out, w_mem_kv, memq_norm, memk_norm, w_gate_up, w_down, mla_w_in, mla_q_a_norm, mla_w_q_b, mla_kv_a_norm, mla_w_kv_b, mla_q_norm, mla_k_norm, gqa_w_in, gqa_q_norm, gqa_k_norm):
    batch, seq, _ = x.shape
    depth = norm_mix.shape[0]
    n_rows = batch * seq
    tm_proj = _row_tile(n_rows, seq, 512)
    tk = _row_tile(n_rows, seq, 512)
    tq = _row_tile(n_rows, seq, 1024)
    tm_post = _row_tile(n_rows, seq, 512)
    tm_ffn = _row_tile(n_rows, seq, 512)
    mem_scale = HEAD ** -0.5 * LOG2E

    k_mem, v_mem = _memkv(mem, mem_norm, w_mem_kv, memk_norm)
    x2 = x.reshape(n_rows, D_MODEL)
    for i in range(depth):
        j = i // 2
        g_mix = norm_mix[i].reshape(1, D_MODEL)
        mq = memq_norm[i].reshape(1, HEAD)
        if i % 2 == 0:
            w_in = mla_w_in[j]
            o_qm = MLA_Q_RANK + MLA_KV_RANK + MLA_ROPE
            w_in = jnp.concatenate(
                [w_in[:, :o_qm], jnp.zeros((D_MODEL, HEAD - MLA_ROPE), F32), w_in[:, o_qm:]], axis=1)
            w_qb = mla_w_q_b[j].reshape(MLA_Q_RANK, MLA_HEADS, MLA_NOPE + MLA_ROPE)
            w_qb = jnp.concatenate(
                [w_qb[:, :, :MLA_NOPE].reshape(MLA_Q_RANK, -1),
                 jnp.pad(w_qb[:, :, MLA_NOPE:], ((0, 0), (0, 0), (0, HEAD - MLA_ROPE))
                         ).reshape(MLA_Q_RANK, -1)], axis=1)
            w_kvb = mla_w_kv_b[j].reshape(MLA_KV_RANK, MLA_HEADS, 2 * HEAD)
            w_kvb = jnp.concatenate([w_kvb[:, :, :HEAD].reshape(MLA_KV_RANK, -1),
                                     w_kvb[:, :, HEAD:].reshape(MLA_KV_RANK, -1)], axis=1)
            consts = [g_mix, w_in.astype(BF16),
                      mla_q_a_norm[j].reshape(1, -1), w_qb.astype(BF16),
                      mla_kv_a_norm[j].reshape(1, -1), w_kvb.astype(BF16),
                      _expand_rows(mla_q_norm[j][:MLA_NOPE], tm_proj),
                      _expand_rows(mla_q_norm[j][MLA_NOPE:], tm_proj),
                      _pad_lanes(mla_k_norm[j][:MLA_NOPE]), _pad_lanes(mla_k_norm[j][MLA_NOPE:]),
                      mq]
            kern = functools.partial(_proj_mla_kernel, mem_scale=mem_scale,
                                     qk_scale=(MLA_NOPE + MLA_ROPE) ** -0.5 * LOG2E)
            qt, k2, vt, qm2 = _proj_call(
                kern, x2, seq, consts, _rope_tables(seq, MLA_ROPE),
                MLA_HEADS, MLA_QK, MLA_HEADS, tm_proj, tq, tk)
        else:
            consts = [g_mix, gqa_w_in[j].astype(BF16),
                      _expand_rows(gqa_q_norm[j], tm_proj), gqa_k_norm[j].reshape(1, HEAD), mq]
            kern = functools.partial(_proj_gqa_kernel, mem_scale=mem_scale,
                                     qk_scale=HEAD ** -0.5 * LOG2E)
            qt, k2, vt, qm2 = _proj_call(
                kern, x2, seq, consts, _rope_tables(seq, HEAD),
                GQA_HEADS, HEAD, GQA_KV_HEADS, tm_proj, tq, tk)
        mix = _attention(qt, k2.reshape(batch, seq, -1), vt)
        x2 = _post(x2, mix.reshape(n_rows, -1), qm2, k_mem[i], v_mem[i],
                   w_out[i].astype(BF16), seq, tm_post)
        x2 = _ffn(x2, norm_ffn[i], w_gate_up[i].astype(BF16), w_down[i].astype(BF16), tm_ffn)
    return x2.reshape(batch, seq, D_MODEL)
```

```python
import functools

import jax
import jax.numpy as jnp
from jax import lax
from jax.experimental import pallas as pl
from jax.experimental.pallas import tpu as pltpu

D_MODEL = 1024
GRID_W = 64
ROPE_THETA = 10000.0
EPS = 1e-6
LOG2E = 1.4426950408889634

MEM_HEADS = 4
HEAD = 128
MEM_W = MEM_HEADS * HEAD

MLA_HEADS = 8
MLA_Q_RANK = 384
MLA_KV_RANK = 256
MLA_NOPE = 128
MLA_ROPE = 64
MLA_QK = 2 * HEAD

GQA_HEADS = 8
GQA_KV_HEADS = 2

D_FF = 2816
FF_CHUNK = 256
COL_TILE = 256
EXP_ROWS = 64
SUM_ROWS = 16
UNROLL = 4

VMEM_LIMIT = 56 * 1024 * 1024

BF16 = jnp.bfloat16
F32 = jnp.float32


def _params(n_axes, flags=None):
    return pltpu.CompilerParams(dimension_semantics=("arbitrary",) * n_axes,
                                vmem_limit_bytes=VMEM_LIMIT, flags=flags)


def _rms(x, width=None):
    n = x.shape[-1] if width is None else width
    ms = jnp.sum(x * x, axis=-1, keepdims=True) * (1.0 / n)
    return x * lax.rsqrt(ms + EPS)


def _rope(x, cos, sin, half):
    lane = lax.broadcasted_iota(jnp.int32, x.shape, 1)
    upper = pltpu.roll(x, HEAD - half, axis=1)
    lower = pltpu.roll(x, half, axis=1)
    swapped = jnp.where((lane & (2 * half - 1)) < half, upper, lower)
    return x * cos + swapped * sin


def _rms_t(xt, width=None):
    n = xt.shape[0] if width is None else width
    ms = jnp.sum(xt * xt, axis=0, keepdims=True) * (1.0 / n)
    return xt * lax.rsqrt(ms + EPS)


def _rope_t(xt, cos_t, sin_t, half):
    blocks = [xt[r:r + half] for r in range(0, xt.shape[0], half)]
    swapped = jnp.concatenate([blocks[b ^ 1] for b in range(len(blocks))], axis=0)
    return xt * cos_t + swapped * sin_t


def _nt_dot(a, b):
    return lax.dot_general(a, b, (((1,), (1,)), ((), ())), preferred_element_type=F32)


def _memkv_kernel(mem_ref, g_ref, w_ref, kn_ref, k_ref, v_ref):
    mem_n = (_rms(mem_ref[...]) * g_ref[...]).astype(BF16)
    kv = jnp.dot(mem_n, w_ref[...], preferred_element_type=F32)
    for h in range(MEM_HEADS):
        kh = _rms(kv[:, h * HEAD:(h + 1) * HEAD]) * kn_ref[...]
        k_ref[:, h * HEAD:(h + 1) * HEAD] = kh.astype(BF16)
    v_ref[...] = kv[:, MEM_W:].astype(BF16)


def _memkv(mem, mem_norm, w_mem_kv, memk_norm):
    depth = w_mem_kv.shape[0]
    batch, n_mem, _ = mem.shape
    out = jax.ShapeDtypeStruct((depth, batch, n_mem, MEM_W), BF16)
    return pl.pallas_call(
        _memkv_kernel,
        grid=(depth, batch),
        in_specs=[
            pl.BlockSpec((None, n_mem, D_MODEL), lambda l, b: (b, 0, 0)),
            pl.BlockSpec((1, D_MODEL), lambda l, b: (0, 0)),
            pl.BlockSpec((None, D_MODEL, 2 * MEM_W), lambda l, b: (l, 0, 0)),
            pl.BlockSpec((None, 1, HEAD), lambda l, b: (l, 0, 0)),
        ],
        out_specs=[
            pl.BlockSpec((None, None, n_mem, MEM_W), lambda l, b: (l, b, 0, 0)),
            pl.BlockSpec((None, None, n_mem, MEM_W), lambda l, b: (l, b, 0, 0)),
        ],
        out_shape=[out, out],
        compiler_params=_params(2),
        name="mem_kv",
    )(mem, mem_norm.reshape(1, D_MODEL), w_mem_kv.astype(BF16),
      memk_norm.reshape(depth, 1, HEAD))


def _store_vt(vt_ref, head, v):
    tm = v.shape[0]
    vt_ref[head, :HEAD, :] = v.T.astype(BF16)
    row = lax.broadcasted_iota(jnp.int32, (SUM_ROWS, tm), 0)
    vt_ref[head, HEAD:, :] = jnp.where(row == 0, 1.0, 0.0).astype(BF16)


def _proj_mla_kernel(x_ref, g_ref, w_in_ref, qa_ref, w_qb_ref, kva_ref, w_kvb_ref,
                     qn_ref, qp_ref, kn_ref, kp_ref, mq_ref, cos_ref, sin_ref, cos_t_ref, sin_t_ref,
                     qt_ref, k_ref, vt_ref, qm_ref, *, qk_scale, mem_scale):
    h = (_rms(x_ref[...]) * g_ref[...]).astype(BF16)
    proj = jnp.dot(h, w_in_ref[...], preferred_element_type=F32)
    o_kv = MLA_Q_RANK
    o_pe = o_kv + MLA_KV_RANK
    o_qm = o_pe + HEAD
    c_q = (_rms(proj[:, :o_kv]) * qa_ref[...]).astype(BF16)
    c_kv = (_rms(proj[:, o_kv:o_pe]) * kva_ref[...]).astype(BF16)
    q = jnp.dot(c_q, w_qb_ref[...], preferred_element_type=F32)
    kv = jnp.dot(c_kv, w_kvb_ref[...], preferred_element_type=F32)
    half = MLA_ROPE // 4
    k_pe = _rope(_rms(proj[:, o_pe:o_qm], MLA_ROPE) * kp_ref[...],
                 cos_ref[...], sin_ref[...], half).astype(BF16)
    cos_t = cos_t_ref[...]
    sin_t = sin_t_ref[...]
    n_w = MLA_HEADS * HEAD
    for i in range(MLA_HEADS):
        lo = i * HEAD
        q_n = _rms_t(q[:, lo:lo + HEAD].T) * qn_ref[...] * qk_scale
        q_p = _rope_t(_rms_t(q[:, n_w + lo:n_w + lo + HEAD].T, MLA_ROPE) * qp_ref[...],
                      cos_t, sin_t, half)
        qt_ref[i, :HEAD, :] = q_n.astype(BF16)
        qt_ref[i, HEAD:, :] = (q_p * qk_scale).astype(BF16)
        k_n = _rms(kv[:, lo:lo + HEAD]) * kn_ref[...]
        k_ref[:, 2 * lo:2 * lo + HEAD] = k_n.astype(BF16)
        k_ref[:, 2 * lo + HEAD:2 * lo + 2 * HEAD] = k_pe
    for i in range(MLA_HEADS):
        lo = n_w + i * HEAD
        _store_vt(vt_ref, i, kv[:, lo:lo + HEAD])
    for i in range(MEM_HEADS):
        lo = i * HEAD
        q_m = _rms(proj[:, o_qm + lo:o_qm + lo + HEAD]) * mq_ref[...] * mem_scale
        qm_ref[:, lo:lo + HEAD] = q_m.astype(BF16)


def _proj_gqa_kernel(x_ref, g_ref, w_in_ref, qn_ref, kn_ref, mq_ref, cos_ref, sin_ref,
                     cos_t_ref, sin_t_ref, qt_ref, k_ref, vt_ref, qm_ref, *, qk_scale, mem_scale):
    h = (_rms(x_ref[...]) * g_ref[...]).astype(BF16)
    proj = jnp.dot(h, w_in_ref[...], preferred_element_type=F32)
    cos = cos_ref[...]
    sin = sin_ref[...]
    cos_t = cos_t_ref[...]
    sin_t = sin_t_ref[...]
    half = HEAD // 4
    o_k = GQA_HEADS * HEAD
    o_v = o_k + GQA_KV_HEADS * HEAD
    o_qm = o_v + GQA_KV_HEADS * HEAD
    for i in range(GQA_HEADS):
        lo = i * HEAD
        q_h = _rope_t(_rms_t(proj[:, lo:lo + HEAD].T) * qn_ref[...], cos_t, sin_t, half)
        qt_ref[i] = (q_h * qk_scale).astype(BF16)
    for i in range(GQA_KV_HEADS):
        lo = i * HEAD
        k_h = _rope(_rms(proj[:, o_k + lo:o_k + lo + HEAD]) * kn_ref[...], cos, sin, half)
        k_ref[:, lo:lo + HEAD] = k_h.astype(BF16)
    for i in range(GQA_KV_HEADS):
        lo = o_v + i * HEAD
        _store_vt(vt_ref, i, proj[:, lo:lo + HEAD])
    for i in range(MEM_HEADS):
        lo = i * HEAD
        q_m = _rms(proj[:, o_qm + lo:o_qm + lo + HEAD]) * mq_ref[...] * mem_scale
        qm_ref[:, lo:lo + HEAD] = q_m.astype(BF16)


def _row_tile(n_rows, seq, want):
    t = min(want, seq)
    assert seq % t == 0 and n_rows % t == 0
    return t


def _full(shape):
    return pl.BlockSpec(shape, lambda i: (0,) * len(shape))


def _proj_call(kernel, x2, seq, consts, tables, n_heads, dk, n_kv_heads, tm, tq, tk):
    n_rows = x2.shape[0]
    batch = n_rows // seq
    n_pos = seq // tm
    row = lambda w: pl.BlockSpec((tm, w), lambda i: (i, 0))
    tab = pl.BlockSpec((tm, HEAD), lambda i: (i % n_pos, 0))
    tab_t = pl.BlockSpec((HEAD, tm), lambda i: (0, i % n_pos))
    cos, sin = tables
    rows_out = lambda w: jax.ShapeDtypeStruct((n_rows, w), BF16)
    per_block = tq // tm
    qt_spec = pl.BlockSpec(
        (None, n_heads, None, dk, tm),
        lambda i: (i // n_pos, 0, (i % n_pos) // per_block, 0, (i % n_pos) % per_block))
    qt_out = jax.ShapeDtypeStruct((batch, n_heads, seq // tq, dk, tq), BF16)
    per_chunk = tk // tm
    vt_spec = pl.BlockSpec(
        (None, n_kv_heads, None, HEAD + SUM_ROWS, tm),
        lambda i: (i // n_pos, 0, (i % n_pos) // per_chunk, 0, (i % n_pos) % per_chunk))
    vt_out = jax.ShapeDtypeStruct((batch, n_kv_heads, seq // tk, HEAD + SUM_ROWS, tk), BF16)
    return pl.pallas_call(
        kernel,
        grid=(n_rows // tm,),
        in_specs=[row(D_MODEL)] + [_full(c.shape) for c in consts] + [tab, tab, tab_t, tab_t],
        out_specs=[qt_spec, row(n_kv_heads * dk), vt_spec, row(MEM_W)],
        out_shape=[qt_out, rows_out(n_kv_heads * dk), vt_out, rows_out(MEM_W)],
        compiler_params=_params(1),
        name=kernel.func.__name__.strip("_"),
    )(x2, *consts, cos, sin, cos.T, sin.T)


def _attn_kernel(qt_ref, k_ref, vt_ref, o_ref, s_ref, mx_ref, p_ref, a_ref, m_ref, acc_ref, *, tk):
    n_q, _, tq = qt_ref.shape
    n_chunks = vt_ref.shape[0]
    dv = o_ref.shape[1]
    total = n_q * n_chunks
    shift = n_chunks.bit_length() - 1
    col_tiles = [slice(c, c + COL_TILE) for c in range(0, tq, COL_TILE)]

    def scores(g, slot, cols):
        g = jnp.minimum(g, total - 1)
        start = pl.multiple_of((g & (n_chunks - 1)) * tk, tk)
        s = jnp.dot(k_ref[pl.ds(start, tk), :], qt_ref[g >> shift, :, cols],
                    preferred_element_type=F32)
        s_ref[slot, :, cols] = s
        mx_ref[slot, :, cols] = jnp.max(s.reshape(tk // 8, 8, COL_TILE), axis=0)

    def softmax(g, slot, p_slot, cols):
        first_chunk = (g & (n_chunks - 1)) == 0
        m_old = jnp.where(first_chunk, -jnp.inf, m_ref[:, cols])
        m_new = jnp.maximum(m_old, jnp.max(mx_ref[slot, :, cols], axis=0, keepdims=True))
        m_ref[:, cols] = m_new
        a_ref[p_slot, :, cols] = jnp.exp2(m_old - m_new)
        for r in range(0, tk, EXP_ROWS):
            x = (s_ref[slot, r:r + EXP_ROWS, cols] - m_new).astype(BF16)
            p_ref[p_slot, r:r + EXP_ROWS, cols] = jnp.exp2(x)

    def accumulate(g, slot, cols):
        g = jnp.maximum(g, 0)
        par = (g >> shift) & 1
        pv = jnp.dot(vt_ref[g & (n_chunks - 1)], p_ref[slot, :, cols],
                     preferred_element_type=F32)
        acc_ref[par, :, cols] = acc_ref[par, :, cols] * a_ref[slot, :, cols] + pv

    def finalize(qb):
        par = qb & 1
        rows = pl.ds(pl.multiple_of(qb * tq, tq), tq)
        out = acc_ref[par, :dv, :] / acc_ref[par, dv:dv + 1, :]
        o_ref[rows, :] = out.T.astype(o_ref.dtype)

    def half_step(g, u):
        for cols in col_tiles:
            softmax(g, u % 2, u % 4, cols)
            scores(g + 1, (u + 1) % 2, cols)
            accumulate(g - 2, (u + 2) % 4, cols)

    acc_ref[...] = jnp.zeros(acc_ref.shape, F32)
    p_ref[2:] = jnp.zeros((2,) + p_ref.shape[1:], BF16)
    a_ref[2:] = jnp.ones((2,) + a_ref.shape[1:], F32)
    for cols in col_tiles:
        scores(0, 0, cols)

    def step(t, carry):
        g = UNROLL * t
        for u in range(UNROLL):
            half_step(g + u, u)

        @pl.when(jnp.logical_and(g > 0, (g & (n_chunks - 1)) == 0))
        def _():
            finalize((g >> shift) - 1)

        return carry

    lax.fori_loop(0, total // UNROLL, step, 0)
    for cols in col_tiles:
        accumulate(total - 2, 2, cols)
        accumulate(total - 1, 3, cols)
    finalize(n_q - 1)


def _attention(qt, k, vt):
    batch, n_heads, n_q, dk, tq = qt.shape
    _, n_kv_heads, n_chunks, dv_ext, tk = vt.shape
    seq = n_q * tq
    dv = dv_ext - SUM_ROWS
    group = n_heads // n_kv_heads
    assert n_chunks & (n_chunks - 1) == 0 and n_chunks % UNROLL == 0
    return pl.pallas_call(
        functools.partial(_attn_kernel, tk=tk),
        grid=(batch, n_heads),
        in_specs=[
            pl.BlockSpec((None, None, n_q, dk, tq), lambda b, h: (b, h, 0, 0, 0)),
            pl.BlockSpec((None, seq, dk), lambda b, h: (b, 0, h // group)),
            pl.BlockSpec((None, None, n_chunks, dv_ext, tk), lambda b, h: (b, h // group, 0, 0, 0)),
        ],
        out_specs=pl.BlockSpec((None, seq, dv), lambda b, h: (b, 0, h)),
        out_shape=jax.ShapeDtypeStruct((batch, seq, n_heads * dv), BF16),
        scratch_shapes=[
            pltpu.VMEM((2, tk, tq), F32),
            pltpu.VMEM((2, 8, tq), F32),
            pltpu.VMEM((4, tk, tq), BF16),
            pltpu.VMEM((4, 1, tq), F32),
            pltpu.VMEM((1, tq), F32),
            pltpu.VMEM((2, dv_ext, tq), F32),
        ],
        compiler_params=_params(2),
        name="attention",
    )(qt, k, vt)


def _post_kernel(x_ref, mix_ref, qm_ref, km_ref, vm_ref, w_ref, o_ref):
    n_mix = mix_ref.shape[-1]
    y = x_ref[...] + jnp.dot(mix_ref[...], w_ref[:n_mix, :], preferred_element_type=F32)
    o_mem = []
    for i in range(MEM_HEADS):
        lo = i * HEAD
        s = _nt_dot(qm_ref[:, lo:lo + HEAD], km_ref[:, lo:lo + HEAD])
        e = jnp.exp2(s - jnp.max(s, axis=-1, keepdims=True))
        pv = jnp.dot(e.astype(BF16), vm_ref[:, lo:lo + HEAD], preferred_element_type=F32)
        o_mem.append((pv / jnp.sum(e, axis=-1, keepdims=True)).astype(BF16))
    o_mem = jnp.concatenate(o_mem, axis=-1)
    o_ref[...] = y + jnp.dot(o_mem, w_ref[n_mix:, :], preferred_element_type=F32)


def _post(x2, mix2, qm2, k_mem, v_mem, w_out, seq, tm):
    n_rows = x2.shape[0]
    per_batch = seq // tm
    n_mem = k_mem.shape[1]
    row = lambda w: pl.BlockSpec((tm, w), lambda i: (i, 0))
    memspec = pl.BlockSpec((None, n_mem, MEM_W), lambda i: (i // per_batch, 0, 0))
    return pl.pallas_call(
        _post_kernel,
        grid=(n_rows // tm,),
        in_specs=[row(D_MODEL), row(mix2.shape[1]), row(MEM_W), memspec, memspec,
                  _full(w_out.shape)],
        out_specs=row(D_MODEL),
        out_shape=jax.ShapeDtypeStruct((n_rows, D_MODEL), F32),
        compiler_params=_params(1),
        name="mix_out",
    )(x2, mix2, qm2, k_mem, v_mem, w_out)


def _ffn_kernel(x_ref, g_ref, w_gu_ref, w_down_ref, o_ref):
    x = x_ref[...]
    h = (_rms(x) * g_ref[...]).astype(BF16)
    y = x
    for c in range(D_FF // FF_CHUNK):
        lo = c * FF_CHUNK
        gate = jnp.dot(h, w_gu_ref[:, lo:lo + FF_CHUNK], preferred_element_type=F32)
        up = jnp.dot(h, w_gu_ref[:, D_FF + lo:D_FF + lo + FF_CHUNK], preferred_element_type=F32)
        act = (gate * jax.nn.sigmoid(gate) * up).astype(BF16)
        y = y + jnp.dot(act, w_down_ref[lo:lo + FF_CHUNK, :], preferred_element_type=F32)
    o_ref[...] = y


def _ffn(x2, g, w_gu, w_down, tm):
    n_rows = x2.shape[0]
    row = pl.BlockSpec((tm, D_MODEL), lambda i: (i, 0))
    return pl.pallas_call(
        _ffn_kernel,
        grid=(n_rows // tm,),
        in_specs=[row, _full((1, D_MODEL)), _full(w_gu.shape), _full(w_down.shape)],
        out_specs=row,
        out_shape=jax.ShapeDtypeStruct((n_rows, D_MODEL), F32),
        compiler_params=_params(1),
        name="ffn",
    )(x2, g.reshape(1, D_MODEL), w_gu, w_down)


def _rope_tables(seq, dim):
    rows = seq // GRID_W
    row = jnp.repeat(jnp.arange(rows, dtype=F32), GRID_W)
    col = jnp.tile(jnp.arange(GRID_W, dtype=F32), rows)
    axis_dim = dim // 2
    inv = ROPE_THETA ** (-jnp.arange(0, axis_dim, 2, dtype=F32) / axis_dim)
    ang_r = row[:, None] * inv
    ang_c = col[:, None] * inv
    cos = jnp.concatenate([jnp.cos(ang_r)] * 2 + [jnp.cos(ang_c)] * 2, axis=-1)
    sin = jnp.concatenate([-jnp.sin(ang_r), jnp.sin(ang_r), -jnp.sin(ang_c), jnp.sin(ang_c)], axis=-1)
    pad = ((0, 0), (0, HEAD - dim))
    return jnp.pad(cos, pad), jnp.pad(sin, pad)


def _pad_lanes(v, width=HEAD):
    return jnp.pad(v, (0, width - v.shape[0])).reshape(1, width)


def _expand_rows(v, n_cols):
    return jnp.broadcast_to(jnp.pad(v, (0, HEAD - v.shape[0]))[:, None], (HEAD, n_cols))


def kernel(x, mem, mem_norm, norm_mix, norm_ffn, w_out, w_mem_kv, memq_norm, memk_norm, w_gate_up, w_down, mla_w_in, mla_q_a_norm, mla_w_q_b, mla_kv_a_norm, mla_w_kv_b, mla_q_norm, mla_k_norm, gqa_w_in, gqa_q_norm, gqa_k_norm):
    batch, seq, _ = x.shape
    depth = norm_mix.shape[0]
    n_rows = batch * seq
    tm_proj = _row_tile(n_rows, seq, 512)
    tk = _row_tile(n_rows, seq, 512)
    tq = _row_tile(n_rows, seq, 1024)
    tm_post = _row_tile(n_rows, seq, 512)
    tm_ffn = _row_tile(n_rows, seq, 512)
    mem_scale = HEAD ** -0.5 * LOG2E

    k_mem, v_mem = _memkv(mem, mem_norm, w_mem_kv, memk_norm)
    x2 = x.reshape(n_rows, D_MODEL)
    for i in range(depth):
        j = i // 2
        g_mix = norm_mix[i].reshape(1, D_MODEL)
        mq = memq_norm[i].reshape(1, HEAD)
        if i % 2 == 0:
            w_in = mla_w_in[j]
            o_qm = MLA_Q_RANK + MLA_KV_RANK + MLA_ROPE
            w_in = jnp.concatenate(
                [w_in[:, :o_qm], jnp.zeros((D_MODEL, HEAD - MLA_ROPE), F32), w_in[:, o_qm:]], axis=1)
            w_qb = mla_w_q_b[j].reshape(MLA_Q_RANK, MLA_HEADS, MLA_NOPE + MLA_ROPE)
            w_qb = jnp.concatenate(
                [w_qb[:, :, :MLA_NOPE].reshape(MLA_Q_RANK, -1),
                 jnp.pad(w_qb[:, :, MLA_NOPE:], ((0, 0), (0, 0), (0, HEAD - MLA_ROPE))
                         ).reshape(MLA_Q_RANK, -1)], axis=1)
            w_kvb = mla_w_kv_b[j].reshape(MLA_KV_RANK, MLA_HEADS, 2 * HEAD)
            w_kvb = jnp.concatenate([w_kvb[:, :, :HEAD].reshape(MLA_KV_RANK, -1),
                                     w_kvb[:, :, HEAD:].reshape(MLA_KV_RANK, -1)], axis=1)
            consts = [g_mix, w_in.astype(BF16),
                      mla_q_a_norm[j].reshape(1, -1), w_qb.astype(BF16),
                      mla_kv_a_norm[j].reshape(1, -1), w_kvb.astype(BF16),
                      _expand_rows(mla_q_norm[j][:MLA_NOPE], tm_proj),
                      _expand_rows(mla_q_norm[j][MLA_NOPE:], tm_proj),
                      _pad_lanes(mla_k_norm[j][:MLA_NOPE]), _pad_lanes(mla_k_norm[j][MLA_NOPE:]),
                      mq]
            kern = functools.partial(_proj_mla_kernel, mem_scale=mem_scale,
                                     qk_scale=(MLA_NOPE + MLA_ROPE) ** -0.5 * LOG2E)
            qt, k2, vt, qm2 = _proj_call(
                kern, x2, seq, consts, _rope_tables(seq, MLA_ROPE),
                MLA_HEADS, MLA_QK, MLA_HEADS, tm_proj, tq, tk)
        else:
            consts = [g_mix, gqa_w_in[j].astype(BF16),
                      _expand_rows(gqa_q_norm[j], tm_proj), gqa_k_norm[j].reshape(1, HEAD), mq]
            kern = functools.partial(_proj_gqa_kernel, mem_scale=mem_scale,
                                     qk_scale=HEAD ** -0.5 * LOG2E)
            qt, k2, vt, qm2 = _proj_call(
                kern, x2, seq, consts, _rope_tables(seq, HEAD),
                GQA_HEADS, HEAD, GQA_KV_HEADS, tm_proj, tq, tk)
        mix = _attention(qt, k2.reshape(batch, seq, -1), vt)
        x2 = _post(x2, mix.reshape(n_rows, -1), qm2, k_mem[i], v_mem[i],
                   w_out[i].astype(BF16), seq, tm_post)
        x2 = _ffn(x2, norm_ffn[i], w_gate_up[i].astype(BF16), w_down[i].astype(BF16), tm_ffn)
    return x2.reshape(batch, seq, D_MODEL)
```

```python
import functools

import jax
import jax.numpy as jnp
from jax import lax
from jax.experimental import pallas as pl
from jax.experimental.pallas import tpu as pltpu

D_MODEL = 1024
GRID_W = 64
ROPE_THETA = 10000.0
EPS = 1e-6
LOG2E = 1.4426950408889634

MEM_HEADS = 4
HEAD = 128
MEM_W = MEM_HEADS * HEAD

MLA_HEADS = 8
MLA_Q_RANK = 384
MLA_KV_RANK = 256
MLA_NOPE = 128
MLA_ROPE = 64
MLA_QK = 2 * HEAD

GQA_HEADS = 8
GQA_KV_HEADS = 2

D_FF = 2816
FF_CHUNK = 256
COL_TILE = 256
EXP_ROWS = 64
SUM_ROWS = 16
UNROLL = 4

VMEM_LIMIT = 56 * 1024 * 1024

BF16 = jnp.bfloat16
F32 = jnp.float32


def _params(n_axes, flags=None):
    return pltpu.CompilerParams(dimension_semantics=("arbitrary",) * n_axes,
                                vmem_limit_bytes=VMEM_LIMIT, flags=flags)


def _rms(x, width=None):
    n = x.shape[-1] if width is None else width
    ms = jnp.sum(x * x, axis=-1, keepdims=True) * (1.0 / n)
    return x * lax.rsqrt(ms + EPS)


def _rope(x, cos, sin, half):
    lane = lax.broadcasted_iota(jnp.int32, x.shape, 1)
    upper = pltpu.roll(x, HEAD - half, axis=1)
    lower = pltpu.roll(x, half, axis=1)
    swapped = jnp.where((lane & (2 * half - 1)) < half, upper, lower)
    return x * cos + swapped * sin


def _rms_t(xt, width=None):
    n = xt.shape[0] if width is None else width
    ms = jnp.sum(xt * xt, axis=0, keepdims=True) * (1.0 / n)
    return xt * lax.rsqrt(ms + EPS)


def _rope_t(xt, cos_t, sin_t, half):
    blocks = [xt[r:r + half] for r in range(0, xt.shape[0], half)]
    swapped = jnp.concatenate([blocks[b ^ 1] for b in range(len(blocks))], axis=0)
    return xt * cos_t + swapped * sin_t


def _nt_dot(a, b):
    return lax.dot_general(a, b, (((1,), (1,)), ((), ())), preferred_element_type=F32)


def _memkv_kernel(mem_ref, g_ref, w_ref, kn_ref, k_ref, v_ref):
    mem_n = (_rms(mem_ref[...]) * g_ref[...]).astype(BF16)
    kv = jnp.dot(mem_n, w_ref[...], preferred_element_type=F32)
    for h in range(MEM_HEADS):
        kh = _rms(kv[:, h * HEAD:(h + 1) * HEAD]) * kn_ref[...]
        k_ref[:, h * HEAD:(h + 1) * HEAD] = kh.astype(BF16)
    v_ref[...] = kv[:, MEM_W:].astype(BF16)


def _memkv(mem, mem_norm, w_mem_kv, memk_norm):
    depth = w_mem_kv.shape[0]
    batch, n_mem, _ = mem.shape
    out = jax.ShapeDtypeStruct((depth, batch, n_mem, MEM_W), BF16)
    return pl.pallas_call(
        _memkv_kernel,
        grid=(depth, batch),
        in_specs=[
            pl.BlockSpec((None, n_mem, D_MODEL), lambda l, b: (b, 0, 0)),
            pl.BlockSpec((1, D_MODEL), lambda l, b: (0, 0)),
            pl.BlockSpec((None, D_MODEL, 2 * MEM_W), lambda l, b: (l, 0, 0)),
            pl.BlockSpec((None, 1, HEAD), lambda l, b: (l, 0, 0)),
        ],
        out_specs=[
            pl.BlockSpec((None, None, n_mem, MEM_W), lambda l, b: (l, b, 0, 0)),
            pl.BlockSpec((None, None, n_mem, MEM_W), lambda l, b: (l, b, 0, 0)),
        ],
        out_shape=[out, out],
        compiler_params=_params(2),
        name="mem_kv",
    )(mem, mem_norm.reshape(1, D_MODEL), w_mem_kv.astype(BF16),
      memk_norm.reshape(depth, 1, HEAD))


def _store_vt(vt_ref, head, v):
    tm = v.shape[0]
    vt_ref[head, :HEAD, :] = v.T.astype(BF16)
    row = lax.broadcasted_iota(jnp.int32, (SUM_ROWS, tm), 0)
    vt_ref[head, HEAD:, :] = jnp.where(row == 0, 1.0, 0.0).astype(BF16)


def _proj_mla_kernel(x_ref, g_ref, w_in_ref, qa_ref, w_qb_ref, kva_ref, w_kvb_ref,
                     qn_ref, qp_ref, kn_ref, kp_ref, mq_ref, cos_ref, sin_ref, cos_t_ref, sin_t_ref,
                     qt_ref, k_ref, vt_ref, qm_ref, *, qk_scale, mem_scale):
    h = (_rms(x_ref[...]) * g_ref[...]).astype(BF16)
    proj = jnp.dot(h, w_in_ref[...], preferred_element_type=F32)
    o_kv = MLA_Q_RANK
    o_pe = o_kv + MLA_KV_RANK
    o_qm = o_pe + HEAD
    c_q = (_rms(proj[:, :o_kv]) * qa_ref[...]).astype(BF16)
    c_kv = (_rms(proj[:, o_kv:o_pe]) * kva_ref[...]).astype(BF16)
    q = jnp.dot(c_q, w_qb_ref[...], preferred_element_type=F32)
    kv = jnp.dot(c_kv, w_kvb_ref[...], preferred_element_type=F32)
    half = MLA_ROPE // 4
    k_pe = _rope(_rms(proj[:, o_pe:o_qm], MLA_ROPE) * kp_ref[...],
                 cos_ref[...], sin_ref[...], half).astype(BF16)
    cos_t = cos_t_ref[...]
    sin_t = sin_t_ref[...]
    n_w = MLA_HEADS * HEAD
    for i in range(MLA_HEADS):
        lo = i * HEAD
        q_n = _rms_t(q[:, lo:lo + HEAD].T) * qn_ref[...] * qk_scale
        q_p = _rope_t(_rms_t(q[:, n_w + lo:n_w + lo + HEAD].T, MLA_ROPE) * qp_ref[...],
                      cos_t, sin_t, half)
        qt_ref[i, :HEAD, :] = q_n.astype(BF16)
        qt_ref[i, HEAD:, :] = (q_p * qk_scale).astype(BF16)
        k_n = _rms(kv[:, lo:lo + HEAD]) * kn_ref[...]
        k_ref[i, :, :HEAD] = k_n.astype(BF16)
        k_ref[i, :, HEAD:] = k_pe
    for i in range(MLA_HEADS):
        lo = n_w + i * HEAD
        _store_vt(vt_ref, i, kv[:, lo:lo + HEAD])
    for i in range(MEM_HEADS):
        lo = i * HEAD
        q_m = _rms(proj[:, o_qm + lo:o_qm + lo + HEAD]) * mq_ref[...] * mem_scale
        qm_ref[:, lo:lo + HEAD] = q_m.astype(BF16)


def _proj_gqa_kernel(x_ref, g_ref, w_in_ref, qn_ref, kn_ref, mq_ref, cos_ref, sin_ref,
                     cos_t_ref, sin_t_ref, qt_ref, k_ref, vt_ref, qm_ref, *, qk_scale, mem_scale):
    h = (_rms(x_ref[...]) * g_ref[...]).astype(BF16)
    proj = jnp.dot(h, w_in_ref[...], preferred_element_type=F32)
    cos = cos_ref[...]
    sin = sin_ref[...]
    cos_t = cos_t_ref[...]
    sin_t = sin_t_ref[...]
    half = HEAD // 4
    o_k = GQA_HEADS * HEAD
    o_v = o_k + GQA_KV_HEADS * HEAD
    o_qm = o_v + GQA_KV_HEADS * HEAD
    for i in range(GQA_HEADS):
        lo = i * HEAD
        q_h = _rope_t(_rms_t(proj[:, lo:lo + HEAD].T) * qn_ref[...], cos_t, sin_t, half)
        qt_ref[i] = (q_h * qk_scale).astype(BF16)
    for i in range(GQA_KV_HEADS):
        lo = i * HEAD
        k_h = _rope(_rms(proj[:, o_k + lo:o_k + lo + HEAD]) * kn_ref[...], cos, sin, half)
        k_ref[i] = k_h.astype(BF16)
    for i in range(GQA_KV_HEADS):
        lo = o_v + i * HEAD
        _store_vt(vt_ref, i, proj[:, lo:lo + HEAD])
    for i in range(MEM_HEADS):
        lo = i * HEAD
        q_m = _rms(proj[:, o_qm + lo:o_qm + lo + HEAD]) * mq_ref[...] * mem_scale
        qm_ref[:, lo:lo + HEAD] = q_m.astype(BF16)


def _row_tile(n_rows, seq, want):
    t = min(want, seq)
    assert seq % t == 0 and n_rows % t == 0
    return t


def _full(shape):
    return pl.BlockSpec(shape, lambda i: (0,) * len(shape))


def _proj_call(kernel, x2, seq, consts, tables, n_heads, dk, n_kv_heads, tm, tq, tk):
    n_rows = x2.shape[0]
    batch = n_rows // seq
    n_pos = seq // tm
    row = lambda w: pl.BlockSpec((tm, w), lambda i: (i, 0))
    tab = pl.BlockSpec((tm, HEAD), lambda i: (i % n_pos, 0))
    tab_t = pl.BlockSpec((HEAD, tm), lambda i: (0, i % n_pos))
    cos, sin = tables
    rows_out = lambda w: jax.ShapeDtypeStruct((n_rows, w), BF16)
    per_block = tq // tm
    qt_spec = pl.BlockSpec(
        (None, n_heads, None, dk, tm),
        lambda i: (i // n_pos, 0, (i % n_pos) // per_block, 0, (i % n_pos) % per_block))
    qt_out = jax.ShapeDtypeStruct((batch, n_heads, seq // tq, dk, tq), BF16)
    k_spec = pl.BlockSpec((None, n_kv_heads, tm, dk), lambda i: (i // n_pos, 0, i % n_pos, 0))
    k_out = jax.ShapeDtypeStruct((batch, n_kv_heads, seq, dk), BF16)
    per_chunk = tk // tm
    vt_spec = pl.BlockSpec(
        (None, n_kv_heads, None, HEAD + SUM_ROWS, tm),
        lambda i: (i // n_pos, 0, (i % n_pos) // per_chunk, 0, (i % n_pos) % per_chunk))
    vt_out = jax.ShapeDtypeStruct((batch, n_kv_heads, seq // tk, HEAD + SUM_ROWS, tk), BF16)
    return pl.pallas_call(
        kernel,
        grid=(n_rows // tm,),
        in_specs=[row(D_MODEL)] + [_full(c.shape) for c in consts] + [tab, tab, tab_t, tab_t],
        out_specs=[qt_spec, k_spec, vt_spec, row(MEM_W)],
        out_shape=[qt_out, k_out, vt_out, rows_out(MEM_W)],
        compiler_params=_params(1),
        name=kernel.func.__name__.strip("_"),
    )(x2, *consts, cos, sin, cos.T, sin.T)


def _attn_kernel(qt_ref, k_ref, vt_ref, o_ref, s_ref, mx_ref, p_ref, a_ref, m_ref, acc_ref, *, tk):
    n_q, _, tq = qt_ref.shape
    n_chunks = vt_ref.shape[0]
    dv = o_ref.shape[1]
    total = n_q * n_chunks
    shift = n_chunks.bit_length() - 1
    col_tiles = [slice(c, c + COL_TILE) for c in range(0, tq, COL_TILE)]

    def scores(g, slot, cols):
        g = jnp.minimum(g, total - 1)
        start = pl.multiple_of((g & (n_chunks - 1)) * tk, tk)
        s = jnp.dot(k_ref[pl.ds(start, tk), :], qt_ref[g >> shift, :, cols],
                    preferred_element_type=F32)
        s_ref[slot, :, cols] = s
        mx_ref[slot, :, cols] = jnp.max(s.reshape(tk // 8, 8, COL_TILE), axis=0)

    def softmax(g, slot, p_slot, cols):
        first_chunk = (g & (n_chunks - 1)) == 0
        m_old = jnp.where(first_chunk, -jnp.inf, m_ref[:, cols])
        m_new = jnp.maximum(m_old, jnp.max(mx_ref[slot, :, cols], axis=0, keepdims=True))
        m_ref[:, cols] = m_new
        a_ref[p_slot, :, cols] = jnp.exp2(m_old - m_new)
        for r in range(0, tk, EXP_ROWS):
            p = jnp.exp2(s_ref[slot, r:r + EXP_ROWS, cols] - m_new)
            p_ref[p_slot, r:r + EXP_ROWS, cols] = p.astype(BF16)

    def accumulate(g, slot, cols):
        g = jnp.maximum(g, 0)
        par = (g >> shift) & 1
        pv = jnp.dot(vt_ref[g & (n_chunks - 1)], p_ref[slot, :, cols],
                     preferred_element_type=F32)
        acc_ref[par, :, cols] = acc_ref[par, :, cols] * a_ref[slot, :, cols] + pv

    def finalize(qb):
        par = qb & 1
        rows = pl.ds(pl.multiple_of(qb * tq, tq), tq)
        out = acc_ref[par, :dv, :] / acc_ref[par, dv:dv + 1, :]
        o_ref[rows, :] = out.T.astype(o_ref.dtype)

    def half_step(g, u):
        for cols in col_tiles:
            softmax(g, u % 2, u % 4, cols)
            scores(g + 1, (u + 1) % 2, cols)
            accumulate(g - 2, (u + 2) % 4, cols)

    acc_ref[...] = jnp.zeros(acc_ref.shape, F32)
    p_ref[2:] = jnp.zeros((2,) + p_ref.shape[1:], BF16)
    a_ref[2:] = jnp.ones((2,) + a_ref.shape[1:], F32)
    for cols in col_tiles:
        scores(0, 0, cols)

    def step(t, carry):
        g = UNROLL * t
        for u in range(UNROLL):
            half_step(g + u, u)

        @pl.when(jnp.logical_and(g > 0, (g & (n_chunks - 1)) == 0))
        def _():
            finalize((g >> shift) - 1)

        return carry

    lax.fori_loop(0, total // UNROLL, step, 0)
    for cols in col_tiles:
        accumulate(total - 2, 2, cols)
        accumulate(total - 1, 3, cols)
    finalize(n_q - 1)


def _attention(qt, k, vt):
    batch, n_heads, n_q, dk, tq = qt.shape
    _, n_kv_heads, n_chunks, dv_ext, tk = vt.shape
    seq = n_q * tq
    dv = dv_ext - SUM_ROWS
    group = n_heads // n_kv_heads
    assert n_chunks & (n_chunks - 1) == 0 and n_chunks % UNROLL == 0
    return pl.pallas_call(
        functools.partial(_attn_kernel, tk=tk),
        grid=(batch, n_heads),
        in_specs=[
            pl.BlockSpec((None, None, n_q, dk, tq), lambda b, h: (b, h, 0, 0, 0)),
            pl.BlockSpec((None, None, seq, dk), lambda b, h: (b, h // group, 0, 0)),
            pl.BlockSpec((None, None, n_chunks, dv_ext, tk), lambda b, h: (b, h // group, 0, 0, 0)),
        ],
        out_specs=pl.BlockSpec((None, None, seq, dv), lambda b, h: (b, h, 0, 0)),
        out_shape=jax.ShapeDtypeStruct((batch, n_heads, seq, dv), BF16),
        scratch_shapes=[
            pltpu.VMEM((2, tk, tq), F32),
            pltpu.VMEM((2, 8, tq), F32),
            pltpu.VMEM((4, tk, tq), BF16),
            pltpu.VMEM((4, 1, tq), F32),
            pltpu.VMEM((1, tq), F32),
            pltpu.VMEM((2, dv_ext, tq), F32),
        ],
        compiler_params=_params(2),
        name="attention",
    )(qt, k, vt)


def _post_kernel(x_ref, mix_ref, qm_ref, km_ref, vm_ref, w_ref, o_ref):
    mix = jnp.concatenate([mix_ref[h] for h in range(mix_ref.shape[0])], axis=-1)
    n_mix = mix.shape[-1]
    y = x_ref[...] + jnp.dot(mix, w_ref[:n_mix, :], preferred_element_type=F32)
    o_mem = []
    for i in range(MEM_HEADS):
        lo = i * HEAD
        s = _nt_dot(qm_ref[:, lo:lo + HEAD], km_ref[:, lo:lo + HEAD])
        e = jnp.exp2(s - jnp.max(s, axis=-1, keepdims=True))
        pv = jnp.dot(e.astype(BF16), vm_ref[:, lo:lo + HEAD], preferred_element_type=F32)
        o_mem.append((pv / jnp.sum(e, axis=-1, keepdims=True)).astype(BF16))
    o_mem = jnp.concatenate(o_mem, axis=-1)
    o_ref[...] = y + jnp.dot(o_mem, w_ref[n_mix:, :], preferred_element_type=F32)


def _post(x2, mix, qm2, k_mem, v_mem, w_out, seq, tm):
    n_rows = x2.shape[0]
    per_batch = seq // tm
    n_mem = k_mem.shape[1]
    _, n_heads, _, dv = mix.shape
    row = lambda w: pl.BlockSpec((tm, w), lambda i: (i, 0))
    memspec = pl.BlockSpec((None, n_mem, MEM_W), lambda i: (i // per_batch, 0, 0))
    mixspec = pl.BlockSpec((None, n_heads, tm, dv),
                           lambda i: (i // per_batch, 0, i % per_batch, 0))
    return pl.pallas_call(
        _post_kernel,
        grid=(n_rows // tm,),
        in_specs=[row(D_MODEL), mixspec, row(MEM_W), memspec, memspec, _full(w_out.shape)],
        out_specs=row(D_MODEL),
        out_shape=jax.ShapeDtypeStruct((n_rows, D_MODEL), F32),
        compiler_params=_params(1),
        name="mix_out",
    )(x2, mix, qm2, k_mem, v_mem, w_out)


def _ffn_kernel(x_ref, g_ref, w_gu_ref, w_down_ref, o_ref):
    x = x_ref[...]
    h = (_rms(x) * g_ref[...]).astype(BF16)
    y = x
    for c in range(D_FF // FF_CHUNK):
        lo = c * FF_CHUNK
        gate = jnp.dot(h, w_gu_ref[:, lo:lo + FF_CHUNK], preferred_element_type=F32)
        up = jnp.dot(h, w_gu_ref[:, D_FF + lo:D_FF + lo + FF_CHUNK], preferred_element_type=F32)
        act = (gate * jax.nn.sigmoid(gate) * up).astype(BF16)
        y = y + jnp.dot(act, w_down_ref[lo:lo + FF_CHUNK, :], preferred_element_type=F32)
    o_ref[...] = y


def _ffn(x2, g, w_gu, w_down, tm):
    n_rows = x2.shape[0]
    row = pl.BlockSpec((tm, D_MODEL), lambda i: (i, 0))
    return pl.pallas_call(
        _ffn_kernel,
        grid=(n_rows // tm,),
        in_specs=[row, _full((1, D_MODEL)), _full(w_gu.shape), _full(w_down.shape)],
        out_specs=row,
        out_shape=jax.ShapeDtypeStruct((n_rows, D_MODEL), F32),
        compiler_params=_params(1),
        name="ffn",
    )(x2, g.reshape(1, D_MODEL), w_gu, w_down)


def _rope_tables(seq, dim):
    rows = seq // GRID_W
    row = jnp.repeat(jnp.arange(rows, dtype=F32), GRID_W)
    col = jnp.tile(jnp.arange(GRID_W, dtype=F32), rows)
    axis_dim = dim // 2
    inv = ROPE_THETA ** (-jnp.arange(0, axis_dim, 2, dtype=F32) / axis_dim)
    ang_r = row[:, None] * inv
    ang_c = col[:, None] * inv
    cos = jnp.concatenate([jnp.cos(ang_r)] * 2 + [jnp.cos(ang_c)] * 2, axis=-1)
    sin = jnp.concatenate([-jnp.sin(ang_r), jnp.sin(ang_r), -jnp.sin(ang_c), jnp.sin(ang_c)], axis=-1)
    pad = ((0, 0), (0, HEAD - dim))
    return jnp.pad(cos, pad), jnp.pad(sin, pad)


def _pad_lanes(v, width=HEAD):
    return jnp.pad(v, (0, width - v.shape[0])).reshape(1, width)


def _expand_rows(v, n_cols):
    return jnp.broadcast_to(jnp.pad(v, (0, HEAD - v.shape[0]))[:, None], (HEAD, n_cols))


def kernel(x, mem, mem_norm, norm_mix, norm_ffn, w_out, w_mem_kv, memq_norm, memk_norm, w_gate_up, w_down, mla_w_in, mla_q_a_norm, mla_w_q_b, mla_kv_a_norm, mla_w_kv_b, mla_q_norm, mla_k_norm, gqa_w_in, gqa_q_norm, gqa_k_norm):
    batch, seq, _ = x.shape
    depth = norm_mix.shape[0]
    n_rows = batch * seq
    tm_proj = _row_tile(n_rows, seq, 512)
    tk = _row_tile(n_rows, seq, 512)
    tq = _row_tile(n_rows, seq, 1024)
    tm_post = _row_tile(n_rows, seq, 512)
    tm_ffn = _row_tile(n_rows, seq, 512)
    mem_scale = HEAD ** -0.5 * LOG2E

    k_mem, v_mem = _memkv(mem, mem_norm, w_mem_kv, memk_norm)
    x2 = x.reshape(n_rows, D_MODEL)
    for i in range(depth):
        j = i // 2
        g_mix = norm_mix[i].reshape(1, D_MODEL)
        mq = memq_norm[i].reshape(1, HEAD)
        if i % 2 == 0:
            w_in = mla_w_in[j]
            o_qm = MLA_Q_RANK + MLA_KV_RANK + MLA_ROPE
            w_in = jnp.concatenate(
                [w_in[:, :o_qm], jnp.zeros((D_MODEL, HEAD - MLA_ROPE), F32), w_in[:, o_qm:]], axis=1)
            w_qb = mla_w_q_b[j].reshape(MLA_Q_RANK, MLA_HEADS, MLA_NOPE + MLA_ROPE)
            w_qb = jnp.concatenate(
                [w_qb[:, :, :MLA_NOPE].reshape(MLA_Q_RANK, -1),
                 jnp.pad(w_qb[:, :, MLA_NOPE:], ((0, 0), (0, 0), (0, HEAD - MLA_ROPE))
                         ).reshape(MLA_Q_RANK, -1)], axis=1)
            w_kvb = mla_w_kv_b[j].reshape(MLA_KV_RANK, MLA_HEADS, 2 * HEAD)
            w_kvb = jnp.concatenate([w_kvb[:, :, :HEAD].reshape(MLA_KV_RANK, -1),
                                     w_kvb[:, :, HEAD:].reshape(MLA_KV_RANK, -1)], axis=1)
            consts = [g_mix, w_in.astype(BF16),
                      mla_q_a_norm[j].reshape(1, -1), w_qb.astype(BF16),
                      mla_kv_a_norm[j].reshape(1, -1), w_kvb.astype(BF16),
                      _expand_rows(mla_q_norm[j][:MLA_NOPE], tm_proj),
                      _expand_rows(mla_q_norm[j][MLA_NOPE:], tm_proj),
                      _pad_lanes(mla_k_norm[j][:MLA_NOPE]), _pad_lanes(mla_k_norm[j][MLA_NOPE:]),
                      mq]
            kern = functools.partial(_proj_mla_kernel, mem_scale=mem_scale,
                                     qk_scale=(MLA_NOPE + MLA_ROPE) ** -0.5 * LOG2E)
            qt, k2, vt, qm2 = _proj_call(
                kern, x2, seq, consts, _rope_tables(seq, MLA_ROPE),
                MLA_HEADS, MLA_QK, MLA_HEADS, tm_proj, tq, tk)
        else:
            consts = [g_mix, gqa_w_in[j].astype(BF16),
                      _expand_rows(gqa_q_norm[j], tm_proj), gqa_k_norm[j].reshape(1, HEAD), mq]
            kern = functools.partial(_proj_gqa_kernel, mem_scale=mem_scale,
                                     qk_scale=HEAD ** -0.5 * LOG2E)
            qt, k2, vt, qm2 = _proj_call(
                kern, x2, seq, consts, _rope_tables(seq, HEAD),
                GQA_HEADS, HEAD, GQA_KV_HEADS, tm_proj, tq, tk)
        mix = _attention(qt, k2, vt)
        x2 = _post(x2, mix, qm2, k_mem[i], v_mem[i],
                   w_out[i].astype(BF16), seq, tm_post)
        x2 = _ffn(x2, norm_ffn[i], w_gate_up[i].astype(BF16), w_down[i].astype(BF16), tm_ffn)
    return x2.reshape(batch, seq, D_MODEL)
```

```python
import functools

import jax
import jax.numpy as jnp
from jax import lax
from jax.experimental import pallas as pl
from jax.experimental.pallas import tpu as pltpu

D_MODEL = 1024
GRID_W = 64
ROPE_THETA = 10000.0
EPS = 1e-6
LOG2E = 1.4426950408889634

MEM_HEADS = 4
HEAD = 128
MEM_W = MEM_HEADS * HEAD

MLA_HEADS = 8
MLA_Q_RANK = 384
MLA_KV_RANK = 256
MLA_NOPE = 128
MLA_ROPE = 64
MLA_QK = 2 * HEAD

GQA_HEADS = 8
GQA_KV_HEADS = 2

D_FF = 2816
FF_CHUNK = 256
COL_TILE = 256
EXP_ROWS = 64
SUM_ROWS = 16
UNROLL = 2
PV_LAG = 1
PROJ_SUB = 256

VMEM_LIMIT = 56 * 1024 * 1024

BF16 = jnp.bfloat16
F32 = jnp.float32


def _params(n_axes, flags=None):
    return pltpu.CompilerParams(dimension_semantics=("arbitrary",) * n_axes,
                                vmem_limit_bytes=VMEM_LIMIT, flags=flags)


def _rms(x, width=None):
    n = x.shape[-1] if width is None else width
    ms = jnp.sum(x * x, axis=-1, keepdims=True) * (1.0 / n)
    return x * lax.rsqrt(ms + EPS)


def _rope(x, cos, sin, half):
    lane = lax.broadcasted_iota(jnp.int32, x.shape, 1)
    upper = pltpu.roll(x, HEAD - half, axis=1)
    lower = pltpu.roll(x, half, axis=1)
    swapped = jnp.where((lane & (2 * half - 1)) < half, upper, lower)
    return x * cos + swapped * sin


def _rms_t(xt, width=None):
    n = xt.shape[0] if width is None else width
    ms = jnp.sum(xt * xt, axis=0, keepdims=True) * (1.0 / n)
    return xt * lax.rsqrt(ms + EPS)


def _rope_t(xt, cos_t, sin_t, half):
    blocks = [xt[r:r + half] for r in range(0, xt.shape[0], half)]
    swapped = jnp.concatenate([blocks[b ^ 1] for b in range(len(blocks))], axis=0)
    return xt * cos_t + swapped * sin_t


def _nt_dot(a, b):
    return lax.dot_general(a, b, (((1,), (1,)), ((), ())), preferred_element_type=F32)


def _memkv_kernel(mem_ref, g_ref, w_ref, kn_ref, k_ref, v_ref):
    mem_n = (_rms(mem_ref[...]) * g_ref[...]).astype(BF16)
    kv = jnp.dot(mem_n, w_ref[...], preferred_element_type=F32)
    for h in range(MEM_HEADS):
        kh = _rms(kv[:, h * HEAD:(h + 1) * HEAD]) * kn_ref[...]
        k_ref[:, h * HEAD:(h + 1) * HEAD] = kh.astype(BF16)
    v_ref[...] = kv[:, MEM_W:].astype(BF16)


def _memkv(mem, mem_norm, w_mem_kv, memk_norm):
    depth = w_mem_kv.shape[0]
    batch, n_mem, _ = mem.shape
    out = jax.ShapeDtypeStruct((depth, batch, n_mem, MEM_W), BF16)
    return pl.pallas_call(
        _memkv_kernel,
        grid=(depth, batch),
        in_specs=[
            pl.BlockSpec((None, n_mem, D_MODEL), lambda l, b: (b, 0, 0)),
            pl.BlockSpec((1, D_MODEL), lambda l, b: (0, 0)),
            pl.BlockSpec((None, D_MODEL, 2 * MEM_W), lambda l, b: (l, 0, 0)),
            pl.BlockSpec((None, 1, HEAD), lambda l, b: (l, 0, 0)),
        ],
        out_specs=[
            pl.BlockSpec((None, None, n_mem, MEM_W), lambda l, b: (l, b, 0, 0)),
            pl.BlockSpec((None, None, n_mem, MEM_W), lambda l, b: (l, b, 0, 0)),
        ],
        out_shape=[out, out],
        compiler_params=_params(2),
        name="mem_kv",
    )(mem, mem_norm.reshape(1, D_MODEL), w_mem_kv.astype(BF16),
      memk_norm.reshape(depth, 1, HEAD))


def _store_vt(vt_ref, head, v_t, toks=slice(None)):
    n = v_t.shape[1]
    vt_ref[head, :HEAD, toks] = v_t.astype(BF16)
    row = lax.broadcasted_iota(jnp.int32, (SUM_ROWS, n), 0)
    vt_ref[head, HEAD:, toks] = jnp.where(row == 0, 1.0, 0.0).astype(BF16)


def _proj_mla_kernel(x_ref, g_ref, w_in_ref, qa_ref, w_qbt_ref, kva_ref, w_kb_ref, w_vbt_ref,
                     qn_ref, qp_ref, kn_ref, kp_ref, mq_ref, cos_ref, sin_ref, cos_t_ref, sin_t_ref,
                     qt_ref, k_ref, vt_ref, qm_ref, *, qk_scale, mem_scale):
    o_kv = MLA_Q_RANK
    o_pe = o_kv + MLA_KV_RANK
    o_qm = o_pe + HEAD
    half = MLA_ROPE // 4
    n_w = MLA_HEADS * HEAD
    sub = x_ref.shape[0]
    for r in range(0, x_ref.shape[0], sub):
        toks = slice(r, r + sub)
        h = (_rms(x_ref[toks, :]) * g_ref[...]).astype(BF16)
        proj = jnp.dot(h, w_in_ref[...], preferred_element_type=F32)
        c_q = _rms(proj[:, :o_kv]) * qa_ref[...]
        c_kv = _rms(proj[:, o_kv:o_pe]) * kva_ref[...]
        q_t = jnp.dot(w_qbt_ref[...], c_q.T.astype(BF16),
                      preferred_element_type=F32)
        k = jnp.dot(c_kv.astype(BF16), w_kb_ref[...], preferred_element_type=F32)
        v_t = jnp.dot(w_vbt_ref[...], c_kv.T.astype(BF16),
                      preferred_element_type=F32)
        k_pe = _rope(_rms(proj[:, o_pe:o_qm], MLA_ROPE) * kp_ref[...],
                     cos_ref[toks, :], sin_ref[toks, :], half).astype(BF16)
        cos_t = cos_t_ref[:, toks]
        sin_t = sin_t_ref[:, toks]
        for i in range(MLA_HEADS):
            lo = i * HEAD
            q_n = _rms_t(q_t[lo:lo + HEAD]) * qn_ref[:, toks] * qk_scale
            q_p = _rope_t(_rms_t(q_t[n_w + lo:n_w + lo + HEAD], MLA_ROPE) * qp_ref[:, toks],
                          cos_t, sin_t, half)
            qt_ref[i, :HEAD, toks] = q_n.astype(BF16)
            qt_ref[i, HEAD:, toks] = (q_p * qk_scale).astype(BF16)
            k_n = _rms(k[:, lo:lo + HEAD]) * kn_ref[...]
            k_ref[i, toks, :HEAD] = k_n.astype(BF16)
            k_ref[i, toks, HEAD:] = k_pe
            _store_vt(vt_ref, i, v_t[lo:lo + HEAD], toks)
        for i in range(MEM_HEADS):
            lo = i * HEAD
            q_m = _rms(proj[:, o_qm + lo:o_qm + lo + HEAD]) * mq_ref[...] * mem_scale
            qm_ref[toks, lo:lo + HEAD] = q_m.astype(BF16)


def _proj_gqa_kernel(x_ref, g_ref, w_in_ref, qn_ref, kn_ref, mq_ref, cos_ref, sin_ref,
                     cos_t_ref, sin_t_ref, qt_ref, k_ref, vt_ref, qm_ref, *, qk_scale, mem_scale):
    half = HEAD // 4
    o_k = GQA_HEADS * HEAD
    o_v = o_k + GQA_KV_HEADS * HEAD
    o_qm = o_v + GQA_KV_HEADS * HEAD
    for r in range(0, x_ref.shape[0], PROJ_SUB):
        toks = slice(r, r + PROJ_SUB)
        h = (_rms(x_ref[toks, :]) * g_ref[...]).astype(BF16)
        proj = jnp.dot(h, w_in_ref[...], preferred_element_type=F32)
        cos_t = cos_t_ref[:, toks]
        sin_t = sin_t_ref[:, toks]
        for i in range(GQA_HEADS):
            lo = i * HEAD
            q_h = _rope_t(_rms_t(proj[:, lo:lo + HEAD].T) * qn_ref[:, toks], cos_t, sin_t, half)
            qt_ref[i, :, toks] = (q_h * qk_scale).astype(BF16)
        for i in range(GQA_KV_HEADS):
            lo = o_k + i * HEAD
            k_h = _rope(_rms(proj[:, lo:lo + HEAD]) * kn_ref[...],
                        cos_ref[toks, :], sin_ref[toks, :], half)
            k_ref[i, toks, :] = k_h.astype(BF16)
        for i in range(GQA_KV_HEADS):
            lo = o_v + i * HEAD
            _store_vt(vt_ref, i, proj[:, lo:lo + HEAD].T, toks)
        for i in range(MEM_HEADS):
            lo = o_qm + i * HEAD
            q_m = _rms(proj[:, lo:lo + HEAD]) * mq_ref[...] * mem_scale
            qm_ref[toks, i * HEAD:(i + 1) * HEAD] = q_m.astype(BF16)


def _row_tile(n_rows, seq, want):
    t = min(want, seq)
    assert seq % t == 0 and n_rows % t == 0
    return t


def _full(shape):
    return pl.BlockSpec(shape, lambda i: (0,) * len(shape), pipeline_mode=pl.Buffered(1))


def _proj_call(kernel, x2, seq, consts, tables, n_heads, dk, n_kv_heads, tm, tq, tk):
    n_rows = x2.shape[0]
    batch = n_rows // seq
    n_pos = seq // tm
    row = lambda w: pl.BlockSpec((tm, w), lambda i: (i, 0))
    tab = pl.BlockSpec((tm, HEAD), lambda i: (i % n_pos, 0))
    tab_t = pl.BlockSpec((HEAD, tm), lambda i: (0, i % n_pos))
    cos, sin = tables
    rows_out = lambda w: jax.ShapeDtypeStruct((n_rows, w), BF16)
    per_block = tq // tm
    qt_spec = pl.BlockSpec(
        (None, n_heads, None, dk, tm),
        lambda i: (i // n_pos, 0, (i % n_pos) // per_block, 0, (i % n_pos) % per_block))
    qt_out = jax.ShapeDtypeStruct((batch, n_heads, seq // tq, dk, tq), BF16)
    k_spec = pl.BlockSpec((None, n_kv_heads, tm, dk), lambda i: (i // n_pos, 0, i % n_pos, 0))
    k_out = jax.ShapeDtypeStruct((batch, n_kv_heads, seq, dk), BF16)
    per_chunk = tk // tm
    vt_spec = pl.BlockSpec(
        (None, n_kv_heads, None, HEAD + SUM_ROWS, tm),
        lambda i: (i // n_pos, 0, (i % n_pos) // per_chunk, 0, (i % n_pos) % per_chunk))
    vt_out = jax.ShapeDtypeStruct((batch, n_kv_heads, seq // tk, HEAD + SUM_ROWS, tk), BF16)
    return pl.pallas_call(
        kernel,
        grid=(n_rows // tm,),
        in_specs=[row(D_MODEL)] + [_full(c.shape) for c in consts] + [tab, tab, tab_t, tab_t],
        out_specs=[qt_spec, k_spec, vt_spec, row(MEM_W)],
        out_shape=[qt_out, k_out, vt_out, rows_out(MEM_W)],
        compiler_params=_params(1),
        name=kernel.func.__name__.strip("_"),
    )(x2, *consts, cos, sin, cos.T, sin.T)


def _attn_kernel(qt_ref, k_ref, vt_ref, o_ref, s_ref, mx_ref, p_ref, a_ref, m_ref, acc_ref, *, tk):
    n_q, _, tq = qt_ref.shape
    n_chunks = vt_ref.shape[0]
    dv = o_ref.shape[1]
    total = n_q * n_chunks
    shift = n_chunks.bit_length() - 1
    col_tiles = [slice(c, c + COL_TILE) for c in range(0, tq, COL_TILE)]

    def scores(g, slot, cols):
        g = jnp.minimum(g, total - 1)
        start = pl.multiple_of((g & (n_chunks - 1)) * tk, tk)
        s = jnp.dot(k_ref[pl.ds(start, tk), :], qt_ref[g >> shift, :, cols],
                    preferred_element_type=F32)
        s_ref[slot, :, cols] = s
        mx_ref[slot, :, cols] = jnp.max(s.reshape(tk // 8, 8, COL_TILE), axis=0)

    def softmax(g, slot, p_slot, cols):
        first_chunk = (g & (n_chunks - 1)) == 0
        m_old = jnp.where(first_chunk, -jnp.inf, m_ref[:, cols])
        m_new = jnp.maximum(m_old, jnp.max(mx_ref[slot, :, cols], axis=0, keepdims=True))
        m_ref[:, cols] = m_new
        a_ref[p_slot, :, cols] = jnp.exp2(m_old - m_new)
        for r in range(0, tk, EXP_ROWS):
            p = jnp.exp2(s_ref[slot, r:r + EXP_ROWS, cols] - m_new)
            p_ref[p_slot, r:r + EXP_ROWS, cols] = p.astype(BF16)

    def accumulate(g, slot, cols):
        g = jnp.maximum(g, 0)
        par = (g >> shift) & 1
        pv = jnp.dot(vt_ref[g & (n_chunks - 1)], p_ref[slot, :, cols],
                     preferred_element_type=F32)
        acc_ref[par, :, cols] = acc_ref[par, :, cols] * a_ref[slot, :, cols] + pv

    def finalize(qb):
        par = qb & 1
        rows = pl.ds(pl.multiple_of(qb * tq, tq), tq)
        out = acc_ref[par, :dv, :] / acc_ref[par, dv:dv + 1, :]
        o_ref[rows, :] = out.T.astype(o_ref.dtype)

    p_slots = p_ref.shape[0]
    assert p_slots == 2 * PV_LAG and UNROLL % p_slots == 0

    def half_step(g, u):
        for cols in col_tiles:
            softmax(g, u % 2, u % p_slots, cols)
            scores(g + 1, (u + 1) % 2, cols)
            accumulate(g - PV_LAG, (u - PV_LAG) % p_slots, cols)

    @pl.when(jnp.logical_and(pl.program_id(0) == 0, pl.program_id(1) == 0))
    def _():
        acc_ref[...] = jnp.zeros(acc_ref.shape, F32)
        p_ref[PV_LAG:] = jnp.zeros((PV_LAG,) + p_ref.shape[1:], BF16)

    a_ref[PV_LAG:] = jnp.ones((PV_LAG,) + a_ref.shape[1:], F32)
    for cols in col_tiles:
        scores(0, 0, cols)

    def step(t, carry):
        g = UNROLL * t
        for u in range(UNROLL):
            half_step(g + u, u)

        @pl.when(jnp.logical_and(g > 0, (g & (n_chunks - 1)) == 0))
        def _():
            finalize((g >> shift) - 1)

        return carry

    lax.fori_loop(0, total // UNROLL, step, 0)
    for cols in col_tiles:
        for d in range(PV_LAG, 0, -1):
            accumulate(total - d, (total - d) % p_slots, cols)
    finalize(n_q - 1)


def _attention(qt, k, vt):
    batch, n_heads, n_q, dk, tq = qt.shape
    _, n_kv_heads, n_chunks, dv_ext, tk = vt.shape
    seq = n_q * tq
    dv = dv_ext - SUM_ROWS
    group = n_heads // n_kv_heads
    assert n_chunks & (n_chunks - 1) == 0 and n_chunks % UNROLL == 0
    return pl.pallas_call(
        functools.partial(_attn_kernel, tk=tk),
        grid=(batch, n_heads),
        in_specs=[
            pl.BlockSpec((None, None, n_q, dk, tq), lambda b, h: (b, h, 0, 0, 0)),
            pl.BlockSpec((None, None, seq, dk), lambda b, h: (b, h // group, 0, 0)),
            pl.BlockSpec((None, None, n_chunks, dv_ext, tk), lambda b, h: (b, h // group, 0, 0, 0)),
        ],
        out_specs=pl.BlockSpec((None, None, seq, dv), lambda b, h: (b, h, 0, 0)),
        out_shape=jax.ShapeDtypeStruct((batch, n_heads, seq, dv), BF16),
        scratch_shapes=[
            pltpu.VMEM((2, tk, tq), F32),
            pltpu.VMEM((2, 8, tq), F32),
            pltpu.VMEM((2 * PV_LAG, tk, tq), BF16),
            pltpu.VMEM((2 * PV_LAG, 1, tq), F32),
            pltpu.VMEM((1, tq), F32),
            pltpu.VMEM((2, dv_ext, tq), F32),
        ],
        compiler_params=_params(2),
        name="attention",
    )(qt, k, vt)


def _post_kernel(x_ref, mix_ref, qm_ref, km_ref, vm_ref, w_ref, o_ref):
    mix = jnp.concatenate([mix_ref[h] for h in range(mix_ref.shape[0])], axis=-1)
    n_mix = mix.shape[-1]
    y = x_ref[...] + jnp.dot(mix, w_ref[:n_mix, :], preferred_element_type=F32)
    o_mem = []
    for i in range(MEM_HEADS):
        lo = i * HEAD
        s = _nt_dot(qm_ref[:, lo:lo + HEAD], km_ref[:, lo:lo + HEAD])
        e = jnp.exp2(s - jnp.max(s, axis=-1, keepdims=True))
        pv = jnp.dot(e.astype(BF16), vm_ref[:, lo:lo + HEAD], preferred_element_type=F32)
        o_mem.append((pv / jnp.sum(e, axis=-1, keepdims=True)).astype(BF16))
    o_mem = jnp.concatenate(o_mem, axis=-1)
    o_ref[...] = y + jnp.dot(o_mem, w_ref[n_mix:, :], preferred_element_type=F32)


def _post(x2, mix, qm2, k_mem, v_mem, w_out, seq, tm):
    n_rows = x2.shape[0]
    per_batch = seq // tm
    n_mem = k_mem.shape[1]
    _, n_heads, _, dv = mix.shape
    row = lambda w: pl.BlockSpec((tm, w), lambda i: (i, 0))
    memspec = pl.BlockSpec((None, n_mem, MEM_W), lambda i: (i // per_batch, 0, 0))
    mixspec = pl.BlockSpec((None, n_heads, tm, dv),
                           lambda i: (i // per_batch, 0, i % per_batch, 0))
    return pl.pallas_call(
        _post_kernel,
        grid=(n_rows // tm,),
        in_specs=[row(D_MODEL), mixspec, row(MEM_W), memspec, memspec, _full(w_out.shape)],
        out_specs=row(D_MODEL),
        out_shape=jax.ShapeDtypeStruct((n_rows, D_MODEL), F32),
        compiler_params=_params(1),
        name="mix_out",
    )(x2, mix, qm2, k_mem, v_mem, w_out)


def _ffn_kernel(x_ref, g_ref, w_gu_ref, w_down_ref, o_ref):
    x = x_ref[...]
    h = (_rms(x) * g_ref[...]).astype(BF16)
    y = x
    for c in range(D_FF // FF_CHUNK):
        lo = c * FF_CHUNK
        gate = jnp.dot(h, w_gu_ref[:, lo:lo + FF_CHUNK], preferred_element_type=F32)
        up = jnp.dot(h, w_gu_ref[:, D_FF + lo:D_FF + lo + FF_CHUNK], preferred_element_type=F32)
        act = (gate * jax.nn.sigmoid(gate) * up).astype(BF16)
        y = y + jnp.dot(act, w_down_ref[lo:lo + FF_CHUNK, :], preferred_element_type=F32)
    o_ref[...] = y


def _ffn(x2, g, w_gu, w_down, tm):
    n_rows = x2.shape[0]
    row = pl.BlockSpec((tm, D_MODEL), lambda i: (i, 0))
    return pl.pallas_call(
        _ffn_kernel,
        grid=(n_rows // tm,),
        in_specs=[row, _full((1, D_MODEL)), _full(w_gu.shape), _full(w_down.shape)],
        out_specs=row,
        out_shape=jax.ShapeDtypeStruct((n_rows, D_MODEL), F32),
        compiler_params=_params(1),
        name="ffn",
    )(x2, g.reshape(1, D_MODEL), w_gu, w_down)


def _rope_tables(seq, dim):
    rows = seq // GRID_W
    row = jnp.repeat(jnp.arange(rows, dtype=F32), GRID_W)
    col = jnp.tile(jnp.arange(GRID_W, dtype=F32), rows)
    axis_dim = dim // 2
    inv = ROPE_THETA ** (-jnp.arange(0, axis_dim, 2, dtype=F32) / axis_dim)
    ang_r = row[:, None] * inv
    ang_c = col[:, None] * inv
    cos = jnp.concatenate([jnp.cos(ang_r)] * 2 + [jnp.cos(ang_c)] * 2, axis=-1)
    sin = jnp.concatenate([-jnp.sin(ang_r), jnp.sin(ang_r), -jnp.sin(ang_c), jnp.sin(ang_c)], axis=-1)
    pad = ((0, 0), (0, HEAD - dim))
    return jnp.pad(cos, pad), jnp.pad(sin, pad)


def _pad_lanes(v, width=HEAD):
    return jnp.pad(v, (0, width - v.shape[0])).reshape(1, width)


def _expand_rows(v, n_cols):
    return jnp.broadcast_to(jnp.pad(v, (0, HEAD - v.shape[0]))[:, None], (HEAD, n_cols))


def kernel(x, mem, mem_norm, norm_mix, norm_ffn, w_out, w_mem_kv, memq_norm, memk_norm, w_gate_up, w_down, mla_w_in, mla_q_a_norm, mla_w_q_b, mla_kv_a_norm, mla_w_kv_b, mla_q_norm, mla_k_norm, gqa_w_in, gqa_q_norm, gqa_k_norm):
    batch, seq, _ = x.shape
    depth = norm_mix.shape[0]
    n_rows = batch * seq
    tm_proj = _row_tile(n_rows, seq, 512)
    tk = _row_tile(n_rows, seq, 512)
    tq = _row_tile(n_rows, seq, 2048)
    tm_post = _row_tile(n_rows, seq, 1024)
    tm_ffn = _row_tile(n_rows, seq, 1024)
    mem_scale = HEAD ** -0.5 * LOG2E

    k_mem, v_mem = _memkv(mem, mem_norm, w_mem_kv, memk_norm)
    x2 = x.reshape(n_rows, D_MODEL)
    for i in range(depth):
        j = i // 2
        g_mix = norm_mix[i].reshape(1, D_MODEL)
        mq = memq_norm[i].reshape(1, HEAD)
        if i % 2 == 0:
            w_in = mla_w_in[j]
            o_qm = MLA_Q_RANK + MLA_KV_RANK + MLA_ROPE
            w_in = jnp.concatenate(
                [w_in[:, :o_qm], jnp.zeros((D_MODEL, HEAD - MLA_ROPE), F32), w_in[:, o_qm:]], axis=1)
            w_qb = mla_w_q_b[j].reshape(MLA_Q_RANK, MLA_HEADS, MLA_NOPE + MLA_ROPE)
            w_qb = jnp.concatenate(
                [w_qb[:, :, :MLA_NOPE].reshape(MLA_Q_RANK, -1),
                 jnp.pad(w_qb[:, :, MLA_NOPE:], ((0, 0), (0, 0), (0, HEAD - MLA_ROPE))
                         ).reshape(MLA_Q_RANK, -1)], axis=1)
            w_kvb = mla_w_kv_b[j].reshape(MLA_KV_RANK, MLA_HEADS, 2 * HEAD)
            w_kb = w_kvb[:, :, :HEAD].reshape(MLA_KV_RANK, -1)
            w_vb = w_kvb[:, :, HEAD:].reshape(MLA_KV_RANK, -1)
            consts = [g_mix, w_in.astype(BF16),
                      mla_q_a_norm[j].reshape(1, -1), w_qb.T.astype(BF16),
                      mla_kv_a_norm[j].reshape(1, -1), w_kb.astype(BF16), w_vb.T.astype(BF16),
                      _expand_rows(mla_q_norm[j][:MLA_NOPE], tm_proj),
                      _expand_rows(mla_q_norm[j][MLA_NOPE:], tm_proj),
                      _pad_lanes(mla_k_norm[j][:MLA_NOPE]), _pad_lanes(mla_k_norm[j][MLA_NOPE:]),
                      mq]
            kern = functools.partial(_proj_mla_kernel, mem_scale=mem_scale,
                                     qk_scale=(MLA_NOPE + MLA_ROPE) ** -0.5 * LOG2E)
            qt, k2, vt, qm2 = _proj_call(
                kern, x2, seq, consts, _rope_tables(seq, MLA_ROPE),
                MLA_HEADS, MLA_QK, MLA_HEADS, tm_proj, tq, tk)
        else:
            consts = [g_mix, gqa_w_in[j].astype(BF16),
                      _expand_rows(gqa_q_norm[j], tm_proj), gqa_k_norm[j].reshape(1, HEAD), mq]
            kern = functools.partial(_proj_gqa_kernel, mem_scale=mem_scale,
                                     qk_scale=HEAD ** -0.5 * LOG2E)
            qt, k2, vt, qm2 = _proj_call(
                kern, x2, seq, consts, _rope_tables(seq, HEAD),
                GQA_HEADS, HEAD, GQA_KV_HEADS, tm_proj, tq, tk)
        mix = _attention(qt, k2, vt)
        x2 = _post(x2, mix, qm2, k_mem[i], v_mem[i],
                   w_out[i].astype(BF16), seq, tm_post)
        x2 = _ffn(x2, norm_ffn[i], w_gate_up[i].astype(BF16), w_down[i].astype(BF16), tm_ffn)
    return x2.reshape(batch, seq, D_MODEL)
```

```python
import functools

import jax
import jax.numpy as jnp
from jax import lax
from jax.experimental import pallas as pl
from jax.experimental.pallas import tpu as pltpu

D_MODEL = 1024
GRID_W = 64
ROPE_THETA = 10000.0
EPS = 1e-6
LOG2E = 1.4426950408889634

MEM_HEADS = 4
HEAD = 128
MEM_W = MEM_HEADS * HEAD

MLA_HEADS = 8
MLA_Q_RANK = 384
MLA_KV_RANK = 256
MLA_NOPE = 128
MLA_ROPE = 64
MLA_QK = 2 * HEAD

GQA_HEADS = 8
GQA_KV_HEADS = 2

D_FF = 2816
FF_CHUNK = 256
COL_TILE = 256
EXP_ROWS = 64
SUM_ROWS = 16
UNROLL = 2
PV_LAG = 1
PROJ_SUB = 256

VMEM_LIMIT = 56 * 1024 * 1024

BF16 = jnp.bfloat16
F32 = jnp.float32


def _params(n_axes, flags=None):
    return pltpu.CompilerParams(dimension_semantics=("arbitrary",) * n_axes,
                                vmem_limit_bytes=VMEM_LIMIT, flags=flags)


def _rms(x, width=None):
    n = x.shape[-1] if width is None else width
    ms = jnp.sum(x * x, axis=-1, keepdims=True) * (1.0 / n)
    return x * lax.rsqrt(ms + EPS)


def _rope(x, cos, sin, half):
    lane = lax.broadcasted_iota(jnp.int32, x.shape, 1)
    upper = pltpu.roll(x, HEAD - half, axis=1)
    lower = pltpu.roll(x, half, axis=1)
    swapped = jnp.where((lane & (2 * half - 1)) < half, upper, lower)
    return x * cos + swapped * sin


def _rms_t(xt, width=None):
    n = xt.shape[0] if width is None else width
    ms = jnp.sum(xt * xt, axis=0, keepdims=True) * (1.0 / n)
    return xt * lax.rsqrt(ms + EPS)


def _rope_t(xt, cos_t, sin_t, half):
    blocks = [xt[r:r + half] for r in range(0, xt.shape[0], half)]
    swapped = jnp.concatenate([blocks[b ^ 1] for b in range(len(blocks))], axis=0)
    return xt * cos_t + swapped * sin_t


def _nt_dot(a, b):
    return lax.dot_general(a, b, (((1,), (1,)), ((), ())), preferred_element_type=F32)


def _memkv_kernel(mem_ref, g_ref, w_ref, kn_ref, k_ref, v_ref):
    mem_n = (_rms(mem_ref[...]) * g_ref[...]).astype(BF16)
    kv = jnp.dot(mem_n, w_ref[...], preferred_element_type=F32)
    for h in range(MEM_HEADS):
        kh = _rms(kv[:, h * HEAD:(h + 1) * HEAD]) * kn_ref[...]
        k_ref[:, h * HEAD:(h + 1) * HEAD] = kh.astype(BF16)
    v_ref[...] = kv[:, MEM_W:].astype(BF16)


def _memkv(mem, mem_norm, w_mem_kv, memk_norm):
    depth = w_mem_kv.shape[0]
    batch, n_mem, _ = mem.shape
    out = jax.ShapeDtypeStruct((depth, batch, n_mem, MEM_W), BF16)
    return pl.pallas_call(
        _memkv_kernel,
        grid=(depth, batch),
        in_specs=[
            pl.BlockSpec((None, n_mem, D_MODEL), lambda l, b: (b, 0, 0)),
            pl.BlockSpec((1, D_MODEL), lambda l, b: (0, 0)),
            pl.BlockSpec((None, D_MODEL, 2 * MEM_W), lambda l, b: (l, 0, 0)),
            pl.BlockSpec((None, 1, HEAD), lambda l, b: (l, 0, 0)),
        ],
        out_specs=[
            pl.BlockSpec((None, None, n_mem, MEM_W), lambda l, b: (l, b, 0, 0)),
            pl.BlockSpec((None, None, n_mem, MEM_W), lambda l, b: (l, b, 0, 0)),
        ],
        out_shape=[out, out],
        compiler_params=_params(2),
        name="mem_kv",
    )(mem, mem_norm.reshape(1, D_MODEL), w_mem_kv.astype(BF16),
      memk_norm.reshape(depth, 1, HEAD))


def _store_vt(vt_ref, head, v_t, toks=slice(None)):
    n = v_t.shape[1]
    vt_ref[head, :HEAD, toks] = v_t.astype(BF16)
    row = lax.broadcasted_iota(jnp.int32, (SUM_ROWS, n), 0)
    vt_ref[head, HEAD:, toks] = jnp.where(row == 0, 1.0, 0.0).astype(BF16)


def _proj_mla_kernel(x_ref, g_ref, w_in_ref, qa_ref, w_qbt_ref, kva_ref, w_kb_ref, w_vbt_ref,
                     qn_ref, qp_ref, kn_ref, kp_ref, mq_ref, cos_ref, sin_ref, cos_t_ref, sin_t_ref,
                     qt_ref, k_ref, vt_ref, qm_ref, *, qk_scale, mem_scale):
    o_kv = MLA_Q_RANK
    o_pe = o_kv + MLA_KV_RANK
    o_qm = o_pe + HEAD
    half = MLA_ROPE // 4
    n_w = MLA_HEADS * HEAD
    sub = x_ref.shape[0]
    for r in range(0, x_ref.shape[0], sub):
        toks = slice(r, r + sub)
        h = (_rms(x_ref[toks, :]) * g_ref[...]).astype(BF16)
        proj = jnp.dot(h, w_in_ref[...], preferred_element_type=F32)
        c_q = _rms(proj[:, :o_kv]) * qa_ref[...]
        c_kv = _rms(proj[:, o_kv:o_pe]) * kva_ref[...]
        q_t = jnp.dot(w_qbt_ref[...], c_q.T.astype(BF16),
                      preferred_element_type=F32)
        k = jnp.dot(c_kv.astype(BF16), w_kb_ref[...], preferred_element_type=F32)
        v_t = jnp.dot(w_vbt_ref[...], c_kv.T.astype(BF16),
                      preferred_element_type=F32)
        k_pe = _rope(_rms(proj[:, o_pe:o_qm], MLA_ROPE) * kp_ref[...],
                     cos_ref[toks, :], sin_ref[toks, :], half).astype(BF16)
        cos_t = cos_t_ref[:, toks]
        sin_t = sin_t_ref[:, toks]
        for i in range(MLA_HEADS):
            lo = i * HEAD
            q_n = _rms_t(q_t[lo:lo + HEAD]) * qn_ref[:, toks] * qk_scale
            q_p = _rope_t(_rms_t(q_t[n_w + lo:n_w + lo + HEAD], MLA_ROPE) * qp_ref[:, toks],
                          cos_t, sin_t, half)
            qt_ref[i, :HEAD, toks] = q_n.astype(BF16)
            qt_ref[i, HEAD:, toks] = (q_p * qk_scale).astype(BF16)
            k_n = _rms(k[:, lo:lo + HEAD]) * kn_ref[...]
            k_ref[i, toks, :HEAD] = k_n.astype(BF16)
            k_ref[i, toks, HEAD:] = k_pe
            _store_vt(vt_ref, i, v_t[lo:lo + HEAD], toks)
        for i in range(MEM_HEADS):
            lo = i * HEAD
            q_m = _rms(proj[:, o_qm + lo:o_qm + lo + HEAD]) * mq_ref[...] * mem_scale
            qm_ref[toks, lo:lo + HEAD] = q_m.astype(BF16)


def _proj_gqa_kernel(x_ref, g_ref, w_in_ref, qn_ref, kn_ref, mq_ref, cos_ref, sin_ref,
                     cos_t_ref, sin_t_ref, qt_ref, k_ref, vt_ref, qm_ref, *, qk_scale, mem_scale):
    half = HEAD // 4
    o_k = GQA_HEADS * HEAD
    o_v = o_k + GQA_KV_HEADS * HEAD
    o_qm = o_v + GQA_KV_HEADS * HEAD
    for r in range(0, x_ref.shape[0], PROJ_SUB):
        toks = slice(r, r + PROJ_SUB)
        h = (_rms(x_ref[toks, :]) * g_ref[...]).astype(BF16)
        proj = jnp.dot(h, w_in_ref[...], preferred_element_type=F32)
        cos_t = cos_t_ref[:, toks]
        sin_t = sin_t_ref[:, toks]
        for i in range(GQA_HEADS):
            lo = i * HEAD
            q_h = _rope_t(_rms_t(proj[:, lo:lo + HEAD].T) * qn_ref[:, toks], cos_t, sin_t, half)
            qt_ref[i, :, toks] = (q_h * qk_scale).astype(BF16)
        for i in range(GQA_KV_HEADS):
            lo = o_k + i * HEAD
            k_h = _rope(_rms(proj[:, lo:lo + HEAD]) * kn_ref[...],
                        cos_ref[toks, :], sin_ref[toks, :], half)
            k_ref[i, toks, :] = k_h.astype(BF16)
        for i in range(GQA_KV_HEADS):
            lo = o_v + i * HEAD
            _store_vt(vt_ref, i, proj[:, lo:lo + HEAD].T, toks)
        for i in range(MEM_HEADS):
            lo = o_qm + i * HEAD
            q_m = _rms(proj[:, lo:lo + HEAD]) * mq_ref[...] * mem_scale
            qm_ref[toks, i * HEAD:(i + 1) * HEAD] = q_m.astype(BF16)


def _row_tile(n_rows, seq, want):
    t = min(want, seq)
    assert seq % t == 0 and n_rows % t == 0
    return t


def _full(shape):
    return pl.BlockSpec(shape, lambda i: (0,) * len(shape), pipeline_mode=pl.Buffered(1))


def _proj_call(kernel, x2, seq, consts, tables, n_heads, dk, n_kv_heads, tm, tq, tk):
    n_rows = x2.shape[0]
    batch = n_rows // seq
    n_pos = seq // tm
    row = lambda w: pl.BlockSpec((tm, w), lambda i: (i, 0))
    tab = pl.BlockSpec((tm, HEAD), lambda i: (i % n_pos, 0))
    tab_t = pl.BlockSpec((HEAD, tm), lambda i: (0, i % n_pos))
    cos, sin = tables
    rows_out = lambda w: jax.ShapeDtypeStruct((n_rows, w), BF16)
    per_block = tq // tm
    qt_spec = pl.BlockSpec(
        (None, n_heads, None, dk, tm),
        lambda i: (i // n_pos, 0, (i % n_pos) // per_block, 0, (i % n_pos) % per_block))
    qt_out = jax.ShapeDtypeStruct((batch, n_heads, seq // tq, dk, tq), BF16)
    k_spec = pl.BlockSpec((None, n_kv_heads, tm, dk), lambda i: (i // n_pos, 0, i % n_pos, 0))
    k_out = jax.ShapeDtypeStruct((batch, n_kv_heads, seq, dk), BF16)
    per_chunk = tk // tm
    vt_spec = pl.BlockSpec(
        (None, n_kv_heads, None, HEAD + SUM_ROWS, tm),
        lambda i: (i // n_pos, 0, (i % n_pos) // per_chunk, 0, (i % n_pos) % per_chunk))
    vt_out = jax.ShapeDtypeStruct((batch, n_kv_heads, seq // tk, HEAD + SUM_ROWS, tk), BF16)
    return pl.pallas_call(
        kernel,
        grid=(n_rows // tm,),
        in_specs=[row(D_MODEL)] + [_full(c.shape) for c in consts] + [tab, tab, tab_t, tab_t],
        out_specs=[qt_spec, k_spec, vt_spec, row(MEM_W)],
        out_shape=[qt_out, k_out, vt_out, rows_out(MEM_W)],
        compiler_params=_params(1),
        name=kernel.func.__name__.strip("_"),
    )(x2, *consts, cos, sin, cos.T, sin.T)


def _attn_kernel(qt_ref, k_ref, vt_ref, o_ref, s_ref, mx_ref, p_ref, a_ref, m_ref, acc_ref, *, tk):
    n_q, _, tq = qt_ref.shape
    n_chunks = vt_ref.shape[0]
    dv = o_ref.shape[1]
    total = n_q * n_chunks
    shift = n_chunks.bit_length() - 1
    col_tiles = [slice(c, c + COL_TILE) for c in range(0, tq, COL_TILE)]

    def scores(g, slot, cols):
        g = jnp.minimum(g, total - 1)
        start = pl.multiple_of((g & (n_chunks - 1)) * tk, tk)
        s = jnp.dot(k_ref[pl.ds(start, tk), :], qt_ref[g >> shift, :, cols],
                    preferred_element_type=F32)
        s_ref[slot, :, cols] = s
        mx_ref[slot, :, cols] = jnp.max(s.reshape(tk // 8, 8, COL_TILE), axis=0)

    def softmax(g, slot, p_slot, cols):
        first_chunk = (g & (n_chunks - 1)) == 0
        m_old = jnp.where(first_chunk, -jnp.inf, m_ref[:, cols])
        m_new = jnp.maximum(m_old, jnp.max(mx_ref[slot, :, cols], axis=0, keepdims=True))
        m_ref[:, cols] = m_new
        a_ref[p_slot, :, cols] = jnp.exp2(m_old - m_new)
        for r in range(0, tk, EXP_ROWS):
            p = jnp.exp2(s_ref[slot, r:r + EXP_ROWS, cols] - m_new)
            p_ref[p_slot, r:r + EXP_ROWS, cols] = p.astype(BF16)

    def accumulate(g, slot, cols):
        g = jnp.maximum(g, 0)
        par = (g >> shift) & 1
        pv = jnp.dot(vt_ref[g & (n_chunks - 1)], p_ref[slot, :, cols],
                     preferred_element_type=F32)
        acc_ref[par, :, cols] = acc_ref[par, :, cols] * a_ref[slot, :, cols] + pv

    def finalize(qb):
        par = qb & 1
        rows = pl.ds(pl.multiple_of(qb * tq, tq), tq)
        out = acc_ref[par, :dv, :] / acc_ref[par, dv:dv + 1, :]
        o_ref[rows, :] = out.T.astype(o_ref.dtype)

    p_slots = p_ref.shape[0]
    assert p_slots == 2 * PV_LAG and UNROLL % p_slots == 0

    def half_step(g, u):
        for cols in col_tiles:
            softmax(g, u % 2, u % p_slots, cols)
            scores(g + 1, (u + 1) % 2, cols)
            accumulate(g - PV_LAG, (u - PV_LAG) % p_slots, cols)

    acc_ref[...] = jnp.zeros(acc_ref.shape, F32)
    p_ref[PV_LAG:] = jnp.zeros((PV_LAG,) + p_ref.shape[1:], BF16)
    a_ref[PV_LAG:] = jnp.ones((PV_LAG,) + a_ref.shape[1:], F32)
    for cols in col_tiles:
        scores(0, 0, cols)

    def step(t, carry):
        g = UNROLL * t
        for u in range(UNROLL):
            half_step(g + u, u)

        @pl.when(jnp.logical_and(g > 0, (g & (n_chunks - 1)) == 0))
        def _():
            finalize((g >> shift) - 1)

        return carry

    lax.fori_loop(0, total // UNROLL, step, 0)
    for cols in col_tiles:
        for d in range(PV_LAG, 0, -1):
            accumulate(total - d, (total - d) % p_slots, cols)
    finalize(n_q - 1)


def _attention(qt, k, vt):
    batch, n_heads, n_q, dk, tq = qt.shape
    _, n_kv_heads, n_chunks, dv_ext, tk = vt.shape
    seq = n_q * tq
    dv = dv_ext - SUM_ROWS
    group = n_heads // n_kv_heads
    assert n_chunks & (n_chunks - 1) == 0 and n_chunks % UNROLL == 0
    return pl.pallas_call(
        functools.partial(_attn_kernel, tk=tk),
        grid=(batch, n_heads),
        in_specs=[
            pl.BlockSpec((None, None, n_q, dk, tq), lambda b, h: (b, h, 0, 0, 0)),
            pl.BlockSpec((None, None, seq, dk), lambda b, h: (b, h // group, 0, 0)),
            pl.BlockSpec((None, None, n_chunks, dv_ext, tk), lambda b, h: (b, h // group, 0, 0, 0)),
        ],
        out_specs=pl.BlockSpec((None, None, seq, dv), lambda b, h: (b, h, 0, 0)),
        out_shape=jax.ShapeDtypeStruct((batch, n_heads, seq, dv), BF16),
        scratch_shapes=[
            pltpu.VMEM((2, tk, tq), F32),
            pltpu.VMEM((2, 8, tq), F32),
            pltpu.VMEM((2 * PV_LAG, tk, tq), BF16),
            pltpu.VMEM((2 * PV_LAG, 1, tq), F32),
            pltpu.VMEM((1, tq), F32),
            pltpu.VMEM((2, dv_ext, tq), F32),
        ],
        compiler_params=_params(2),
        name="attention",
    )(qt, k, vt)


def _post_kernel(x_ref, mix_ref, qm_ref, km_ref, vm_ref, w_ref, o_ref):
    mix = jnp.concatenate([mix_ref[h] for h in range(mix_ref.shape[0])], axis=-1)
    n_mix = mix.shape[-1]
    y = x_ref[...] + jnp.dot(mix, w_ref[:n_mix, :], preferred_element_type=F32)
    o_mem = []
    for i in range(MEM_HEADS):
        lo = i * HEAD
        s = _nt_dot(qm_ref[:, lo:lo + HEAD], km_ref[:, lo:lo + HEAD])
        e = jnp.exp2(s - jnp.max(s, axis=-1, keepdims=True))
        pv = jnp.dot(e.astype(BF16), vm_ref[:, lo:lo + HEAD], preferred_element_type=F32)
        o_mem.append((pv / jnp.sum(e, axis=-1, keepdims=True)).astype(BF16))
    o_mem = jnp.concatenate(o_mem, axis=-1)
    o_ref[...] = y + jnp.dot(o_mem, w_ref[n_mix:, :], preferred_element_type=F32)


def _post(x2, mix, qm2, k_mem, v_mem, w_out, seq, tm):
    n_rows = x2.shape[0]
    per_batch = seq // tm
    n_mem = k_mem.shape[1]
    _, n_heads, _, dv = mix.shape
    row = lambda w: pl.BlockSpec((tm, w), lambda i: (i, 0))
    memspec = pl.BlockSpec((None, n_mem, MEM_W), lambda i: (i // per_batch, 0, 0))
    mixspec = pl.BlockSpec((None, n_heads, tm, dv),
                           lambda i: (i // per_batch, 0, i % per_batch, 0))
    return pl.pallas_call(
        _post_kernel,
        grid=(n_rows // tm,),
        in_specs=[row(D_MODEL), mixspec, row(MEM_W), memspec, memspec, _full(w_out.shape)],
        out_specs=row(D_MODEL),
        out_shape=jax.ShapeDtypeStruct((n_rows, D_MODEL), F32),
        compiler_params=_params(1),
        name="mix_out",
    )(x2, mix, qm2, k_mem, v_mem, w_out)


def _ffn_kernel(x_ref, g_ref, w_gu_ref, w_down_ref, o_ref):
    x = x_ref[...]
    h = (_rms(x) * g_ref[...]).astype(BF16)
    y = x
    for c in range(D_FF // FF_CHUNK):
        lo = c * FF_CHUNK
        gate = jnp.dot(h, w_gu_ref[:, lo:lo + FF_CHUNK], preferred_element_type=F32)
        up = jnp.dot(h, w_gu_ref[:, D_FF + lo:D_FF + lo + FF_CHUNK], preferred_element_type=F32)
        act = (gate * jax.nn.sigmoid(gate) * up).astype(BF16)
        y = y + jnp.dot(act, w_down_ref[lo:lo + FF_CHUNK, :], preferred_element_type=F32)
    o_ref[...] = y


def _ffn(x2, g, w_gu, w_down, tm):
    n_rows = x2.shape[0]
    row = pl.BlockSpec((tm, D_MODEL), lambda i: (i, 0))
    return pl.pallas_call(
        _ffn_kernel,
        grid=(n_rows // tm,),
        in_specs=[row, _full((1, D_MODEL)), _full(w_gu.shape), _full(w_down.shape)],
        out_specs=row,
        out_shape=jax.ShapeDtypeStruct((n_rows, D_MODEL), F32),
        compiler_params=_params(1),
        name="ffn",
    )(x2, g.reshape(1, D_MODEL), w_gu, w_down)


def _rope_tables(seq, dim):
    rows = seq // GRID_W
    row = jnp.repeat(jnp.arange(rows, dtype=F32), GRID_W)
    col = jnp.tile(jnp.arange(GRID_W, dtype=F32), rows)
    axis_dim = dim // 2
    inv = ROPE_THETA ** (-jnp.arange(0, axis_dim, 2, dtype=F32) / axis_dim)
    ang_r = row[:, None] * inv
    ang_c = col[:, None] * inv
    cos = jnp.concatenate([jnp.cos(ang_r)] * 2 + [jnp.cos(ang_c)] * 2, axis=-1)
    sin = jnp.concatenate([-jnp.sin(ang_r), jnp.sin(ang_r), -jnp.sin(ang_c), jnp.sin(ang_c)], axis=-1)
    pad = ((0, 0), (0, HEAD - dim))
    return jnp.pad(cos, pad), jnp.pad(sin, pad)


def _pad_lanes(v, width=HEAD):
    return jnp.pad(v, (0, width - v.shape[0])).reshape(1, width)


def _expand_rows(v, n_cols):
    return jnp.broadcast_to(jnp.pad(v, (0, HEAD - v.shape[0]))[:, None], (HEAD, n_cols))


def kernel(x, mem, mem_norm, norm_mix, norm_ffn, w_out, w_mem_kv, memq_norm, memk_norm, w_gate_up, w_down, mla_w_in, mla_q_a_norm, mla_w_q_b, mla_kv_a_norm, mla_w_kv_b, mla_q_norm, mla_k_norm, gqa_w_in, gqa_q_norm, gqa_k_norm):
    batch, seq, _ = x.shape
    depth = norm_mix.shape[0]
    n_rows = batch * seq
    tm_proj = _row_tile(n_rows, seq, 512)
    tk = _row_tile(n_rows, seq, 512)
    tq = _row_tile(n_rows, seq, 2048)
    tm_post = _row_tile(n_rows, seq, 1024)
    tm_ffn = _row_tile(n_rows, seq, 1024)
    mem_scale = HEAD ** -0.5 * LOG2E

    k_mem, v_mem = _memkv(mem, mem_norm, w_mem_kv, memk_norm)
    x2 = x.reshape(n_rows, D_MODEL)
    for i in range(depth):
        j = i // 2
        g_mix = norm_mix[i].reshape(1, D_MODEL)
        mq = memq_norm[i].reshape(1, HEAD)
        if i % 2 == 0:
            w_in = mla_w_in[j]
            o_qm = MLA_Q_RANK + MLA_KV_RANK + MLA_ROPE
            w_in = jnp.concatenate(
                [w_in[:, :o_qm], jnp.zeros((D_MODEL, HEAD - MLA_ROPE), F32), w_in[:, o_qm:]], axis=1)
            w_qb = mla_w_q_b[j].reshape(MLA_Q_RANK, MLA_HEADS, MLA_NOPE + MLA_ROPE)
            w_qb = jnp.concatenate(
                [w_qb[:, :, :MLA_NOPE].reshape(MLA_Q_RANK, -1),
                 jnp.pad(w_qb[:, :, MLA_NOPE:], ((0, 0), (0, 0), (0, HEAD - MLA_ROPE))
                         ).reshape(MLA_Q_RANK, -1)], axis=1)
            w_kvb = mla_w_kv_b[j].reshape(MLA_KV_RANK, MLA_HEADS, 2 * HEAD)
            w_kb = w_kvb[:, :, :HEAD].reshape(MLA_KV_RANK, -1)
            w_vb = w_kvb[:, :, HEAD:].reshape(MLA_KV_RANK, -1)
            consts = [g_mix, w_in.astype(BF16),
                      mla_q_a_norm[j].reshape(1, -1), w_qb.T.astype(BF16),
                      mla_kv_a_norm[j].reshape(1, -1), w_kb.astype(BF16), w_vb.T.astype(BF16),
                      _expand_rows(mla_q_norm[j][:MLA_NOPE], tm_proj),
                      _expand_rows(mla_q_norm[j][MLA_NOPE:], tm_proj),
                      _pad_lanes(mla_k_norm[j][:MLA_NOPE]), _pad_lanes(mla_k_norm[j][MLA_NOPE:]),
                      mq]
            kern = functools.partial(_proj_mla_kernel, mem_scale=mem_scale,
                                     qk_scale=(MLA_NOPE + MLA_ROPE) ** -0.5 * LOG2E)
            qt, k2, vt, qm2 = _proj_call(
                kern, x2, seq, consts, _rope_tables(seq, MLA_ROPE),
                MLA_HEADS, MLA_QK, MLA_HEADS, tm_proj, tq, tk)
        else:
            consts = [g_mix, gqa_w_in[j].astype(BF16),
                      _expand_rows(gqa_q_norm[j], tm_proj), gqa_k_norm[j].reshape(1, HEAD), mq]
            kern = functools.partial(_proj_gqa_kernel, mem_scale=mem_scale,
                                     qk_scale=HEAD ** -0.5 * LOG2E)
            qt, k2, vt, qm2 = _proj_call(
                kern, x2, seq, consts, _rope_tables(seq, HEAD),
                GQA_HEADS, HEAD, GQA_KV_HEADS, tm_proj, tq, tk)
        mix = _attention(qt, k2, vt)
        x2 = _post(x2, mix, qm2, k_mem[i], v_mem[i],
                   w_out[i].astype(BF16), seq, tm_post)
        x2 = _ffn(x2, norm_ffn[i], w_gate_up[i].astype(BF16), w_down[i].astype(BF16), tm_ffn)
    return x2.reshape(batch, seq, D_MODEL)
```

```python
import functools

import jax
import jax.numpy as jnp
from jax import lax
from jax.experimental import pallas as pl
from jax.experimental.pallas import tpu as pltpu

D_MODEL = 1024
GRID_W = 64
ROPE_THETA = 10000.0
EPS = 1e-6
LOG2E = 1.4426950408889634

MEM_HEADS = 4
HEAD = 128
MEM_W = MEM_HEADS * HEAD

MLA_HEADS = 8
MLA_Q_RANK = 384
MLA_KV_RANK = 256
MLA_NOPE = 128
MLA_ROPE = 64
MLA_QK = 2 * HEAD

GQA_HEADS = 8
GQA_KV_HEADS = 2

D_FF = 2816
FF_CHUNK = 256
COL_TILE = 256
EXP_ROWS = 64
SUM_ROWS = 16
UNROLL = 2
PV_LAG = 1
PROJ_SUB = 256

TM_PROJ_MLA = 512
TM_PROJ_GQA = 512
TM_DENSE = 1024
ATTN_TQ = 2048
ATTN_TK = 512

VMEM_LIMIT = 56 * 1024 * 1024

BF16 = jnp.bfloat16
F32 = jnp.float32


def _params(n_axes, flags=None):
    return pltpu.CompilerParams(dimension_semantics=("arbitrary",) * n_axes,
                                vmem_limit_bytes=VMEM_LIMIT, flags=flags)


def _rms(x, width=None):
    n = x.shape[-1] if width is None else width
    ms = jnp.sum(x * x, axis=-1, keepdims=True) * (1.0 / n)
    return x * lax.rsqrt(ms + EPS)


def _rope(x, cos, sin, half):
    lane = lax.broadcasted_iota(jnp.int32, x.shape, 1)
    upper = pltpu.roll(x, HEAD - half, axis=1)
    lower = pltpu.roll(x, half, axis=1)
    swapped = jnp.where((lane & (2 * half - 1)) < half, upper, lower)
    return x * cos + swapped * sin


def _rms_t(xt, width=None):
    n = xt.shape[0] if width is None else width
    ms = jnp.sum(xt * xt, axis=0, keepdims=True) * (1.0 / n)
    return xt * lax.rsqrt(ms + EPS)


def _rope_t(xt, cos_t, sin_t, half):
    blocks = [xt[r:r + half] for r in range(0, xt.shape[0], half)]
    swapped = jnp.concatenate([blocks[b ^ 1] for b in range(len(blocks))], axis=0)
    return xt * cos_t + swapped * sin_t


def _nt_dot(a, b):
    return lax.dot_general(a, b, (((1,), (1,)), ((), ())), preferred_element_type=F32)


def _memkv_kernel(mem_ref, g_ref, w_ref, kn_ref, k_ref, v_ref):
    mem_n = (_rms(mem_ref[...]) * g_ref[...]).astype(BF16)
    kv = jnp.dot(mem_n, w_ref[...], preferred_element_type=F32)
    for h in range(MEM_HEADS):
        kh = _rms(kv[:, h * HEAD:(h + 1) * HEAD]) * kn_ref[...]
        k_ref[:, h * HEAD:(h + 1) * HEAD] = kh.astype(BF16)
    v_ref[...] = kv[:, MEM_W:].astype(BF16)


def _memkv(mem, mem_norm, w_mem_kv, memk_norm):
    depth = w_mem_kv.shape[0]
    batch, n_mem, _ = mem.shape
    out = jax.ShapeDtypeStruct((depth, batch, n_mem, MEM_W), BF16)
    return pl.pallas_call(
        _memkv_kernel,
        grid=(depth, batch),
        in_specs=[
            pl.BlockSpec((None, n_mem, D_MODEL), lambda l, b: (b, 0, 0)),
            pl.BlockSpec((1, D_MODEL), lambda l, b: (0, 0)),
            pl.BlockSpec((None, D_MODEL, 2 * MEM_W), lambda l, b: (l, 0, 0)),
            pl.BlockSpec((None, 1, HEAD), lambda l, b: (l, 0, 0)),
        ],
        out_specs=[
            pl.BlockSpec((None, None, n_mem, MEM_W), lambda l, b: (l, b, 0, 0)),
            pl.BlockSpec((None, None, n_mem, MEM_W), lambda l, b: (l, b, 0, 0)),
        ],
        out_shape=[out, out],
        compiler_params=_params(2),
        name="mem_kv",
    )(mem, mem_norm.reshape(1, D_MODEL), w_mem_kv.astype(BF16),
      memk_norm.reshape(depth, 1, HEAD))


def _store_vt(vt_ref, head, v_t, toks=slice(None)):
    n = v_t.shape[1]
    vt_ref[head, :HEAD, toks] = v_t.astype(BF16)
    row = lax.broadcasted_iota(jnp.int32, (SUM_ROWS, n), 0)
    vt_ref[head, HEAD:, toks] = jnp.where(row == 0, 1.0, 0.0).astype(BF16)


def _proj_mla_kernel(x_ref, g_ref, w_in_ref, qa_ref, w_qbt_ref, kva_ref, w_kb_ref, w_vbt_ref,
                     qn_ref, qp_ref, kn_ref, kp_ref, mq_ref, cos_ref, sin_ref, cos_t_ref, sin_t_ref,
                     qt_ref, k_ref, vt_ref, qm_ref, *, qk_scale, mem_scale):
    o_kv = MLA_Q_RANK
    o_pe = o_kv + MLA_KV_RANK
    o_qm = o_pe + HEAD
    half = MLA_ROPE // 4
    n_w = MLA_HEADS * HEAD
    h = (_rms(x_ref[...]) * g_ref[...]).astype(BF16)
    proj = jnp.dot(h, w_in_ref[...], preferred_element_type=F32)
    c_q = _rms(proj[:, :o_kv]) * qa_ref[...]
    c_kv = _rms(proj[:, o_kv:o_pe]) * kva_ref[...]
    q_t = jnp.dot(w_qbt_ref[...], c_q.T.astype(BF16),
                  preferred_element_type=F32)
    k = jnp.dot(c_kv.astype(BF16), w_kb_ref[...], preferred_element_type=F32)
    v_t = jnp.dot(w_vbt_ref[...], c_kv.T.astype(BF16), preferred_element_type=F32)
    k_pe = _rope(_rms(proj[:, o_pe:o_qm], MLA_ROPE) * kp_ref[...],
                 cos_ref[...], sin_ref[...], half).astype(BF16)
    cos_t = cos_t_ref[...]
    sin_t = sin_t_ref[...]
    for i in range(MLA_HEADS):
        lo = i * HEAD
        q_n = _rms_t(q_t[lo:lo + HEAD]) * qn_ref[...] * qk_scale
        q_p = _rope_t(_rms_t(q_t[n_w + lo:n_w + lo + HEAD], MLA_ROPE) * qp_ref[...],
                      cos_t, sin_t, half)
        qt_ref[i, :HEAD, :] = q_n.astype(BF16)
        qt_ref[i, HEAD:, :] = (q_p * qk_scale).astype(BF16)
        k_n = _rms(k[:, lo:lo + HEAD]) * kn_ref[...]
        k_ref[i, :, :HEAD] = k_n.astype(BF16)
        k_ref[i, :, HEAD:] = k_pe
        _store_vt(vt_ref, i, v_t[lo:lo + HEAD])
    for i in range(MEM_HEADS):
        lo = i * HEAD
        q_m = _rms(proj[:, o_qm + lo:o_qm + lo + HEAD]) * mq_ref[...] * mem_scale
        qm_ref[:, lo:lo + HEAD] = q_m.astype(BF16)


def _proj_gqa_kernel(x_ref, g_ref, w_in_ref, qn_ref, kn_ref, mq_ref, cos_ref, sin_ref,
                     cos_t_ref, sin_t_ref, qt_ref, k_ref, vt_ref, qm_ref, *, qk_scale, mem_scale):
    half = HEAD // 4
    o_k = GQA_HEADS * HEAD
    o_v = o_k + GQA_KV_HEADS * HEAD
    o_qm = o_v + GQA_KV_HEADS * HEAD
    for r in range(0, x_ref.shape[0], PROJ_SUB):
        toks = slice(r, r + PROJ_SUB)
        h = (_rms(x_ref[toks, :]) * g_ref[...]).astype(BF16)
        proj = jnp.dot(h, w_in_ref[...], preferred_element_type=F32)
        cos_t = cos_t_ref[:, toks]
        sin_t = sin_t_ref[:, toks]
        for i in range(GQA_HEADS):
            lo = i * HEAD
            q_h = _rope_t(_rms_t(proj[:, lo:lo + HEAD].T) * qn_ref[:, toks], cos_t, sin_t, half)
            qt_ref[i, :, toks] = (q_h * qk_scale).astype(BF16)
        for i in range(GQA_KV_HEADS):
            lo = o_k + i * HEAD
            k_h = _rope(_rms(proj[:, lo:lo + HEAD]) * kn_ref[...],
                        cos_ref[toks, :], sin_ref[toks, :], half)
            k_ref[i, toks, :] = k_h.astype(BF16)
        for i in range(GQA_KV_HEADS):
            lo = o_v + i * HEAD
            _store_vt(vt_ref, i, proj[:, lo:lo + HEAD].T, toks)
        for i in range(MEM_HEADS):
            lo = o_qm + i * HEAD
            q_m = _rms(proj[:, lo:lo + HEAD]) * mq_ref[...] * mem_scale
            qm_ref[toks, i * HEAD:(i + 1) * HEAD] = q_m.astype(BF16)


def _row_tile(n_rows, seq, want):
    t = min(want, seq)
    assert seq % t == 0 and n_rows % t == 0
    return t


def _full(shape):
    return pl.BlockSpec(shape, lambda i: (0,) * len(shape), pipeline_mode=pl.Buffered(1))


def _proj_call(kernel, x2, seq, consts, tables, n_heads, dk, n_kv_heads, tm, tq, tk):
    n_rows = x2.shape[0]
    batch = n_rows // seq
    n_pos = seq // tm
    row = lambda w: pl.BlockSpec((tm, w), lambda i: (i, 0))
    tab = pl.BlockSpec((tm, HEAD), lambda i: (i % n_pos, 0))
    tab_t = pl.BlockSpec((HEAD, tm), lambda i: (0, i % n_pos))
    cos, sin = tables
    rows_out = lambda w: jax.ShapeDtypeStruct((n_rows, w), BF16)
    per_block = tq // tm
    qt_spec = pl.BlockSpec(
        (None, n_heads, None, dk, tm),
        lambda i: (i // n_pos, 0, (i % n_pos) // per_block, 0, (i % n_pos) % per_block))
    qt_out = jax.ShapeDtypeStruct((batch, n_heads, seq // tq, dk, tq), BF16)
    k_spec = pl.BlockSpec((None, n_kv_heads, tm, dk), lambda i: (i // n_pos, 0, i % n_pos, 0))
    k_out = jax.ShapeDtypeStruct((batch, n_kv_heads, seq, dk), BF16)
    per_chunk = tk // tm
    vt_spec = pl.BlockSpec(
        (None, n_kv_heads, None, HEAD + SUM_ROWS, tm),
        lambda i: (i // n_pos, 0, (i % n_pos) // per_chunk, 0, (i % n_pos) % per_chunk))
    vt_out = jax.ShapeDtypeStruct((batch, n_kv_heads, seq // tk, HEAD + SUM_ROWS, tk), BF16)
    return pl.pallas_call(
        kernel,
        grid=(n_rows // tm,),
        in_specs=[row(D_MODEL)] + [_full(c.shape) for c in consts] + [tab, tab, tab_t, tab_t],
        out_specs=[qt_spec, k_spec, vt_spec, row(MEM_W)],
        out_shape=[qt_out, k_out, vt_out, rows_out(MEM_W)],
        compiler_params=_params(1),
        name=kernel.func.__name__.strip("_"),
    )(x2, *consts, cos, sin, cos.T, sin.T)


def _attn_kernel(qt_ref, k_ref, vt_ref, o_ref, s_ref, mx_ref, p_ref, a_ref, m_ref, acc_ref, *, tk):
    n_q, _, tq = qt_ref.shape
    n_chunks = vt_ref.shape[0]
    dv = o_ref.shape[1]
    total = n_q * n_chunks
    shift = n_chunks.bit_length() - 1
    col_tiles = [slice(c, c + COL_TILE) for c in range(0, tq, COL_TILE)]

    def scores(g, slot, cols):
        g = jnp.minimum(g, total - 1)
        start = pl.multiple_of((g & (n_chunks - 1)) * tk, tk)
        s = jnp.dot(k_ref[pl.ds(start, tk), :], qt_ref[g >> shift, :, cols],
                    preferred_element_type=F32)
        s_ref[slot, :, cols] = s
        mx_ref[slot, :, cols] = jnp.max(s.reshape(tk // 8, 8, COL_TILE), axis=0)

    def softmax(g, slot, p_slot, cols):
        first_chunk = (g & (n_chunks - 1)) == 0
        m_old = jnp.where(first_chunk, -jnp.inf, m_ref[:, cols])
        m_new = jnp.maximum(m_old, jnp.max(mx_ref[slot, :, cols], axis=0, keepdims=True))
        m_ref[:, cols] = m_new
        a_ref[p_slot, :, cols] = jnp.exp2(m_old - m_new)
        for r in range(0, tk, EXP_ROWS):
            p = jnp.exp2(s_ref[slot, r:r + EXP_ROWS, cols] - m_new)
            p_ref[p_slot, r:r + EXP_ROWS, cols] = p.astype(BF16)

    def accumulate(g, slot, cols):
        g = jnp.maximum(g, 0)
        par = (g >> shift) & 1
        pv = jnp.dot(vt_ref[g & (n_chunks - 1)], p_ref[slot, :, cols],
                     preferred_element_type=F32)
        acc_ref[par, :, cols] = acc_ref[par, :, cols] * a_ref[slot, :, cols] + pv

    def finalize(qb):
        par = qb & 1
        rows = pl.ds(pl.multiple_of(qb * tq, tq), tq)
        out = acc_ref[par, :dv, :] / acc_ref[par, dv:dv + 1, :]
        o_ref[rows, :] = out.T.astype(o_ref.dtype)

    p_slots = p_ref.shape[0]
    assert p_slots == 2 * PV_LAG and UNROLL % p_slots == 0

    def half_step(g, u):
        for cols in col_tiles:
            softmax(g, u % 2, u % p_slots, cols)
            scores(g + 1, (u + 1) % 2, cols)
            accumulate(g - PV_LAG, (u - PV_LAG) % p_slots, cols)

    acc_ref[...] = jnp.zeros(acc_ref.shape, F32)
    p_ref[PV_LAG:] = jnp.zeros((PV_LAG,) + p_ref.shape[1:], BF16)
    a_ref[PV_LAG:] = jnp.ones((PV_LAG,) + a_ref.shape[1:], F32)
    for cols in col_tiles:
        scores(0, 0, cols)

    def step(t, carry):
        g = UNROLL * t
        for u in range(UNROLL):
            half_step(g + u, u)

        @pl.when(jnp.logical_and(g > 0, (g & (n_chunks - 1)) == 0))
        def _():
            finalize((g >> shift) - 1)

        return carry

    lax.fori_loop(0, total // UNROLL, step, 0)
    for cols in col_tiles:
        for d in range(PV_LAG, 0, -1):
            accumulate(total - d, (total - d) % p_slots, cols)
    finalize(n_q - 1)


def _attention(qt, k, vt):
    batch, n_heads, n_q, dk, tq = qt.shape
    _, n_kv_heads, n_chunks, dv_ext, tk = vt.shape
    seq = n_q * tq
    dv = dv_ext - SUM_ROWS
    group = n_heads // n_kv_heads
    assert n_chunks & (n_chunks - 1) == 0 and n_chunks % UNROLL == 0
    return pl.pallas_call(
        functools.partial(_attn_kernel, tk=tk),
        grid=(batch, n_heads),
        in_specs=[
            pl.BlockSpec((None, None, n_q, dk, tq), lambda b, h: (b, h, 0, 0, 0)),
            pl.BlockSpec((None, None, seq, dk), lambda b, h: (b, h // group, 0, 0)),
            pl.BlockSpec((None, None, n_chunks, dv_ext, tk), lambda b, h: (b, h // group, 0, 0, 0)),
        ],
        out_specs=pl.BlockSpec((None, None, seq, dv), lambda b, h: (b, h, 0, 0)),
        out_shape=jax.ShapeDtypeStruct((batch, n_heads, seq, dv), BF16),
        scratch_shapes=[
            pltpu.VMEM((2, tk, tq), F32),
            pltpu.VMEM((2, 8, tq), F32),
            pltpu.VMEM((2 * PV_LAG, tk, tq), BF16),
            pltpu.VMEM((2 * PV_LAG, 1, tq), F32),
            pltpu.VMEM((1, tq), F32),
            pltpu.VMEM((2, dv_ext, tq), F32),
        ],
        compiler_params=_params(2),
        name="attention",
    )(qt, k, vt)


def _post_kernel(x_ref, mix_ref, qm_ref, km_ref, vm_ref, w_ref, o_ref):
    mix = jnp.concatenate([mix_ref[h] for h in range(mix_ref.shape[0])], axis=-1)
    n_mix = mix.shape[-1]
    y = x_ref[...] + jnp.dot(mix, w_ref[:n_mix, :], preferred_element_type=F32)
    o_mem = []
    for i in range(MEM_HEADS):
        lo = i * HEAD
        s = _nt_dot(qm_ref[:, lo:lo + HEAD], km_ref[:, lo:lo + HEAD])
        e = jnp.exp2(s - jnp.max(s, axis=-1, keepdims=True))
        pv = jnp.dot(e.astype(BF16), vm_ref[:, lo:lo + HEAD], preferred_element_type=F32)
        o_mem.append((pv / jnp.sum(e, axis=-1, keepdims=True)).astype(BF16))
    o_mem = jnp.concatenate(o_mem, axis=-1)
    o_ref[...] = y + jnp.dot(o_mem, w_ref[n_mix:, :], preferred_element_type=F32)


def _post(x2, mix, qm2, k_mem, v_mem, w_out, seq, tm):
    n_rows = x2.shape[0]
    per_batch = seq // tm
    n_mem = k_mem.shape[1]
    _, n_heads, _, dv = mix.shape
    row = lambda w: pl.BlockSpec((tm, w), lambda i: (i, 0))
    memspec = pl.BlockSpec((None, n_mem, MEM_W), lambda i: (i // per_batch, 0, 0))
    mixspec = pl.BlockSpec((None, n_heads, tm, dv),
                           lambda i: (i // per_batch, 0, i % per_batch, 0))
    return pl.pallas_call(
        _post_kernel,
        grid=(n_rows // tm,),
        in_specs=[row(D_MODEL), mixspec, row(MEM_W), memspec, memspec, _full(w_out.shape)],
        out_specs=row(D_MODEL),
        out_shape=jax.ShapeDtypeStruct((n_rows, D_MODEL), F32),
        compiler_params=_params(1),
        name="mix_out",
    )(x2, mix, qm2, k_mem, v_mem, w_out)


def _ffn_kernel(x_ref, g_ref, w_gu_ref, w_down_ref, o_ref):
    x = x_ref[...]
    h = (_rms(x) * g_ref[...]).astype(BF16)
    y = x
    for c in range(D_FF // FF_CHUNK):
        lo = c * FF_CHUNK
        gate = jnp.dot(h, w_gu_ref[:, lo:lo + FF_CHUNK], preferred_element_type=F32)
        up = jnp.dot(h, w_gu_ref[:, D_FF + lo:D_FF + lo + FF_CHUNK], preferred_element_type=F32)
        act = (gate * jax.nn.sigmoid(gate) * up).astype(BF16)
        y = y + jnp.dot(act, w_down_ref[lo:lo + FF_CHUNK, :], preferred_element_type=F32)
    o_ref[...] = y


def _ffn(x2, g, w_gu, w_down, tm):
    n_rows = x2.shape[0]
    row = pl.BlockSpec((tm, D_MODEL), lambda i: (i, 0))
    return pl.pallas_call(
        _ffn_kernel,
        grid=(n_rows // tm,),
        in_specs=[row, _full((1, D_MODEL)), _full(w_gu.shape), _full(w_down.shape)],
        out_specs=row,
        out_shape=jax.ShapeDtypeStruct((n_rows, D_MODEL), F32),
        compiler_params=_params(1),
        name="ffn",
    )(x2, g.reshape(1, D_MODEL), w_gu, w_down)


def _rope_tables(seq, dim):
    rows = seq // GRID_W
    row = jnp.repeat(jnp.arange(rows, dtype=F32), GRID_W)
    col = jnp.tile(jnp.arange(GRID_W, dtype=F32), rows)
    axis_dim = dim // 2
    inv = ROPE_THETA ** (-jnp.arange(0, axis_dim, 2, dtype=F32) / axis_dim)
    ang_r = row[:, None] * inv
    ang_c = col[:, None] * inv
    cos = jnp.concatenate([jnp.cos(ang_r)] * 2 + [jnp.cos(ang_c)] * 2, axis=-1)
    sin = jnp.concatenate([-jnp.sin(ang_r), jnp.sin(ang_r), -jnp.sin(ang_c), jnp.sin(ang_c)], axis=-1)
    pad = ((0, 0), (0, HEAD - dim))
    return jnp.pad(cos, pad), jnp.pad(sin, pad)


def _pad_lanes(v, width=HEAD):
    return jnp.pad(v, (0, width - v.shape[0])).reshape(1, width)


def _expand_rows(v, n_cols):
    return jnp.broadcast_to(jnp.pad(v, (0, HEAD - v.shape[0]))[:, None], (HEAD, n_cols))


def kernel(x, mem, mem_norm, norm_mix, norm_ffn, w_out, w_mem_kv, memq_norm, memk_norm, w_gate_up, w_down, mla_w_in, mla_q_a_norm, mla_w_q_b, mla_kv_a_norm, mla_w_kv_b, mla_q_norm, mla_k_norm, gqa_w_in, gqa_q_norm, gqa_k_norm):
    batch, seq, _ = x.shape
    depth = norm_mix.shape[0]
    n_rows = batch * seq
    tm_mla = _row_tile(n_rows, seq, TM_PROJ_MLA)
    tm_gqa = _row_tile(n_rows, seq, TM_PROJ_GQA)
    tk = _row_tile(n_rows, seq, ATTN_TK)
    tq = _row_tile(n_rows, seq, ATTN_TQ)
    tm_post = _row_tile(n_rows, seq, TM_DENSE)
    tm_ffn = _row_tile(n_rows, seq, TM_DENSE)
    mem_scale = HEAD ** -0.5 * LOG2E

    k_mem, v_mem = _memkv(mem, mem_norm, w_mem_kv, memk_norm)
    x2 = x.reshape(n_rows, D_MODEL)
    for i in range(depth):
        j = i // 2
        g_mix = norm_mix[i].reshape(1, D_MODEL)
        mq = memq_norm[i].reshape(1, HEAD)
        if i % 2 == 0:
            w_in = mla_w_in[j]
            o_qm = MLA_Q_RANK + MLA_KV_RANK + MLA_ROPE
            w_in = jnp.concatenate(
                [w_in[:, :o_qm], jnp.zeros((D_MODEL, HEAD - MLA_ROPE), F32), w_in[:, o_qm:]], axis=1)
            w_qb = mla_w_q_b[j].reshape(MLA_Q_RANK, MLA_HEADS, MLA_NOPE + MLA_ROPE)
            w_qb = jnp.concatenate(
                [w_qb[:, :, :MLA_NOPE].reshape(MLA_Q_RANK, -1),
                 jnp.pad(w_qb[:, :, MLA_NOPE:], ((0, 0), (0, 0), (0, HEAD - MLA_ROPE))
                         ).reshape(MLA_Q_RANK, -1)], axis=1)
            w_kvb = mla_w_kv_b[j].reshape(MLA_KV_RANK, MLA_HEADS, 2 * HEAD)
            w_kb = w_kvb[:, :, :HEAD].reshape(MLA_KV_RANK, -1)
            w_vb = w_kvb[:, :, HEAD:].reshape(MLA_KV_RANK, -1)
            consts = [g_mix, w_in.astype(BF16),
                      mla_q_a_norm[j].reshape(1, -1), w_qb.T.astype(BF16),
                      mla_kv_a_norm[j].reshape(1, -1), w_kb.astype(BF16), w_vb.T.astype(BF16),
                      _expand_rows(mla_q_norm[j][:MLA_NOPE], tm_mla),
                      _expand_rows(mla_q_norm[j][MLA_NOPE:], tm_mla),
                      _pad_lanes(mla_k_norm[j][:MLA_NOPE]), _pad_lanes(mla_k_norm[j][MLA_NOPE:]),
                      mq]
            kern = functools.partial(_proj_mla_kernel, mem_scale=mem_scale,
                                     qk_scale=(MLA_NOPE + MLA_ROPE) ** -0.5 * LOG2E)
            qt, k2, vt, qm2 = _proj_call(
                kern, x2, seq, consts, _rope_tables(seq, MLA_ROPE),
                MLA_HEADS, MLA_QK, MLA_HEADS, tm_mla, tq, tk)
        else:
            consts = [g_mix, gqa_w_in[j].astype(BF16),
                      _expand_rows(gqa_q_norm[j], tm_gqa), gqa_k_norm[j].reshape(1, HEAD), mq]
            kern = functools.partial(_proj_gqa_kernel, mem_scale=mem_scale,
                                     qk_scale=HEAD ** -0.5 * LOG2E)
            qt, k2, vt, qm2 = _proj_call(
                kern, x2, seq, consts, _rope_tables(seq, HEAD),
                GQA_HEADS, HEAD, GQA_KV_HEADS, tm_gqa, tq, tk)
        mix = _attention(qt, k2, vt)
        x2 = _post(x2, mix, qm2, k_mem[i], v_mem[i],
                   w_out[i].astype(BF16), seq, tm_post)
        x2 = _ffn(x2, norm_ffn[i], w_gate_up[i].astype(BF16), w_down[i].astype(BF16), tm_ffn)
    return x2.reshape(batch, seq, D_MODEL)
```

```python
import functools

import jax
import jax.numpy as jnp
from jax import lax
from jax.experimental import pallas as pl
from jax.experimental.pallas import tpu as pltpu

D_MODEL = 1024
GRID_W = 64
ROPE_THETA = 10000.0
EPS = 1e-6
LOG2E = 1.4426950408889634

MEM_HEADS = 4
HEAD = 128
MEM_W = MEM_HEADS * HEAD

MLA_HEADS = 8
MLA_Q_RANK = 384
MLA_KV_RANK = 256
MLA_NOPE = 128
MLA_ROPE = 64
MLA_QK = 2 * HEAD

GQA_HEADS = 8
GQA_KV_HEADS = 2

D_FF = 2816
FF_CHUNK = 256
COL_TILE = 256
EXP_ROWS = 64
SUM_ROWS = 16
UNROLL = 4
PV_LAG = 2
PROJ_SUB = 256

TM_PROJ_MLA = 512
TM_PROJ_GQA = 512
TM_DENSE = 1024
ATTN_TQ = 2048
ATTN_TK = 512

VMEM_LIMIT = 56 * 1024 * 1024

BF16 = jnp.bfloat16
F32 = jnp.float32


def _params(n_axes):
    return pltpu.CompilerParams(dimension_semantics=("arbitrary",) * n_axes,
                                vmem_limit_bytes=VMEM_LIMIT)


def _rms(x, width=None):
    n = x.shape[-1] if width is None else width
    ms = jnp.sum(x * x, axis=-1, keepdims=True) * (1.0 / n)
    return x * lax.rsqrt(ms + EPS)


def _rope(x, cos, sin, half):
    lane = lax.broadcasted_iota(jnp.int32, x.shape, 1)
    upper = pltpu.roll(x, HEAD - half, axis=1)
    lower = pltpu.roll(x, half, axis=1)
    swapped = jnp.where((lane & (2 * half - 1)) < half, upper, lower)
    return x * cos + swapped * sin


def _rms_t(xt, width=None):
    n = xt.shape[0] if width is None else width
    ms = jnp.sum(xt * xt, axis=0, keepdims=True) * (1.0 / n)
    return xt * lax.rsqrt(ms + EPS)


def _rope_t(xt, cos_t, sin_t, half):
    blocks = [xt[r:r + half] for r in range(0, xt.shape[0], half)]
    swapped = jnp.concatenate([blocks[b ^ 1] for b in range(len(blocks))], axis=0)
    return xt * cos_t + swapped * sin_t


def _nt_dot(a, b):
    return lax.dot_general(a, b, (((1,), (1,)), ((), ())), preferred_element_type=F32)


def _memkv_kernel(mem_ref, g_ref, w_ref, kn_ref, k_ref, v_ref):
    mem_n = (_rms(mem_ref[...]) * g_ref[...]).astype(BF16)
    kv = jnp.dot(mem_n, w_ref[...], preferred_element_type=F32)
    for h in range(MEM_HEADS):
        kh = _rms(kv[:, h * HEAD:(h + 1) * HEAD]) * kn_ref[...]
        k_ref[:, h * HEAD:(h + 1) * HEAD] = kh.astype(BF16)
    v_ref[...] = kv[:, MEM_W:].astype(BF16)


def _memkv(mem, mem_norm, w_mem_kv, memk_norm):
    depth = w_mem_kv.shape[0]
    batch, n_mem, _ = mem.shape
    out = jax.ShapeDtypeStruct((depth, batch, n_mem, MEM_W), BF16)
    return pl.pallas_call(
        _memkv_kernel,
        grid=(depth, batch),
        in_specs=[
            pl.BlockSpec((None, n_mem, D_MODEL), lambda l, b: (b, 0, 0)),
            pl.BlockSpec((1, D_MODEL), lambda l, b: (0, 0)),
            pl.BlockSpec((None, D_MODEL, 2 * MEM_W), lambda l, b: (l, 0, 0)),
            pl.BlockSpec((None, 1, HEAD), lambda l, b: (l, 0, 0)),
        ],
        out_specs=[
            pl.BlockSpec((None, None, n_mem, MEM_W), lambda l, b: (l, b, 0, 0)),
            pl.BlockSpec((None, None, n_mem, MEM_W), lambda l, b: (l, b, 0, 0)),
        ],
        out_shape=[out, out],
        compiler_params=_params(2),
        name="mem_kv",
    )(mem, mem_norm.reshape(1, D_MODEL), w_mem_kv.astype(BF16),
      memk_norm.reshape(depth, 1, HEAD))


def _store_vt(vt_ref, head, v_t, toks=slice(None)):
    n = v_t.shape[1]
    vt_ref[head, :HEAD, toks] = v_t.astype(BF16)
    row = lax.broadcasted_iota(jnp.int32, (SUM_ROWS, n), 0)
    vt_ref[head, HEAD:, toks] = jnp.where(row == 0, 1.0, 0.0).astype(BF16)


def _proj_mla_kernel(x_ref, g_ref, w_in_ref, qa_ref, w_qbt_ref, kva_ref, w_kb_ref, w_vbt_ref,
                     qn_ref, qp_ref, kn_ref, kp_ref, mq_ref, cos_ref, sin_ref, cos_t_ref, sin_t_ref,
                     qt_ref, k_ref, vt_ref, qm_ref, *, qk_scale, mem_scale):
    o_kv = MLA_Q_RANK
    o_pe = o_kv + MLA_KV_RANK
    o_qm = o_pe + HEAD
    half = MLA_ROPE // 4
    n_w = MLA_HEADS * HEAD
    h = (_rms(x_ref[...]) * g_ref[...]).astype(BF16)
    proj = jnp.dot(h, w_in_ref[...], preferred_element_type=F32)
    c_q = _rms(proj[:, :o_kv]) * qa_ref[...]
    c_kv = _rms(proj[:, o_kv:o_pe]) * kva_ref[...]
    q_t = jnp.dot(w_qbt_ref[...], c_q.T.astype(BF16),
                  preferred_element_type=F32)
    k = jnp.dot(c_kv.astype(BF16), w_kb_ref[...], preferred_element_type=F32)
    v_t = jnp.dot(w_vbt_ref[...], c_kv.T.astype(BF16), preferred_element_type=F32)
    k_pe = _rope(_rms(proj[:, o_pe:o_qm], MLA_ROPE) * kp_ref[...],
                 cos_ref[...], sin_ref[...], half).astype(BF16)
    cos_t = cos_t_ref[...]
    sin_t = sin_t_ref[...]
    for i in range(MLA_HEADS):
        lo = i * HEAD
        q_n = _rms_t(q_t[lo:lo + HEAD]) * qn_ref[...] * qk_scale
        q_p = _rope_t(_rms_t(q_t[n_w + lo:n_w + lo + HEAD], MLA_ROPE) * qp_ref[...],
                      cos_t, sin_t, half)
        qt_ref[i, :HEAD, :] = q_n.astype(BF16)
        qt_ref[i, HEAD:, :] = (q_p * qk_scale).astype(BF16)
        k_n = _rms(k[:, lo:lo + HEAD]) * kn_ref[...]
        k_ref[i, :, :HEAD] = k_n.astype(BF16)
        k_ref[i, :, HEAD:] = k_pe
        _store_vt(vt_ref, i, v_t[lo:lo + HEAD])
    for i in range(MEM_HEADS):
        lo = i * HEAD
        q_m = _rms(proj[:, o_qm + lo:o_qm + lo + HEAD]) * mq_ref[...] * mem_scale
        qm_ref[:, lo:lo + HEAD] = q_m.astype(BF16)


def _proj_gqa_kernel(x_ref, g_ref, w_in_ref, qn_ref, kn_ref, mq_ref, cos_ref, sin_ref,
                     cos_t_ref, sin_t_ref, qt_ref, k_ref, vt_ref, qm_ref, *, qk_scale, mem_scale):
    half = HEAD // 4
    o_k = GQA_HEADS * HEAD
    o_v = o_k + GQA_KV_HEADS * HEAD
    o_qm = o_v + GQA_KV_HEADS * HEAD
    for r in range(0, x_ref.shape[0], PROJ_SUB):
        toks = slice(r, r + PROJ_SUB)
        h = (_rms(x_ref[toks, :]) * g_ref[...]).astype(BF16)
        proj = jnp.dot(h, w_in_ref[...], preferred_element_type=F32)
        cos_t = cos_t_ref[:, toks]
        sin_t = sin_t_ref[:, toks]
        for i in range(GQA_HEADS):
            lo = i * HEAD
            q_h = _rope_t(_rms_t(proj[:, lo:lo + HEAD].T) * qn_ref[:, toks], cos_t, sin_t, half)
            qt_ref[i, :, toks] = (q_h * qk_scale).astype(BF16)
        for i in range(GQA_KV_HEADS):
            lo = o_k + i * HEAD
            k_h = _rope(_rms(proj[:, lo:lo + HEAD]) * kn_ref[...],
                        cos_ref[toks, :], sin_ref[toks, :], half)
            k_ref[i, toks, :] = k_h.astype(BF16)
        for i in range(GQA_KV_HEADS):
            lo = o_v + i * HEAD
            _store_vt(vt_ref, i, proj[:, lo:lo + HEAD].T, toks)
        for i in range(MEM_HEADS):
            lo = o_qm + i * HEAD
            q_m = _rms(proj[:, lo:lo + HEAD]) * mq_ref[...] * mem_scale
            qm_ref[toks, i * HEAD:(i + 1) * HEAD] = q_m.astype(BF16)


def _row_tile(n_rows, seq, want):
    t = min(want, seq)
    assert seq % t == 0 and n_rows % t == 0
    return t


def _full(shape):
    return pl.BlockSpec(shape, lambda i: (0,) * len(shape), pipeline_mode=pl.Buffered(1))


def _proj_call(kernel, x2, seq, consts, tables, n_heads, dk, n_kv_heads, tm, tq, tk):
    n_rows = x2.shape[0]
    batch = n_rows // seq
    n_pos = seq // tm
    row = lambda w: pl.BlockSpec((tm, w), lambda i: (i, 0))
    tab = pl.BlockSpec((tm, HEAD), lambda i: (i % n_pos, 0))
    tab_t = pl.BlockSpec((HEAD, tm), lambda i: (0, i % n_pos))
    cos, sin = tables
    rows_out = lambda w: jax.ShapeDtypeStruct((n_rows, w), BF16)
    per_block = tq // tm
    qt_spec = pl.BlockSpec(
        (None, n_heads, None, dk, tm),
        lambda i: (i // n_pos, 0, (i % n_pos) // per_block, 0, (i % n_pos) % per_block))
    qt_out = jax.ShapeDtypeStruct((batch, n_heads, seq // tq, dk, tq), BF16)
    k_spec = pl.BlockSpec((None, n_kv_heads, tm, dk), lambda i: (i // n_pos, 0, i % n_pos, 0))
    k_out = jax.ShapeDtypeStruct((batch, n_kv_heads, seq, dk), BF16)
    per_chunk = tk // tm
    vt_spec = pl.BlockSpec(
        (None, n_kv_heads, None, HEAD + SUM_ROWS, tm),
        lambda i: (i // n_pos, 0, (i % n_pos) // per_chunk, 0, (i % n_pos) % per_chunk))
    vt_out = jax.ShapeDtypeStruct((batch, n_kv_heads, seq // tk, HEAD + SUM_ROWS, tk), BF16)
    return pl.pallas_call(
        kernel,
        grid=(n_rows // tm,),
        in_specs=[row(D_MODEL)] + [_full(c.shape) for c in consts] + [tab, tab, tab_t, tab_t],
        out_specs=[qt_spec, k_spec, vt_spec, row(MEM_W)],
        out_shape=[qt_out, k_out, vt_out, rows_out(MEM_W)],
        compiler_params=_params(1),
        name=kernel.func.__name__.strip("_"),
    )(x2, *consts, cos, sin, cos.T, sin.T)


def _attn_kernel(qt_ref, k_ref, vt_ref, o_ref, s_ref, mx_ref, p_ref, a_ref, m_ref, acc_ref, *, tk):
    n_q, _, tq = qt_ref.shape
    n_chunks = vt_ref.shape[0]
    dv = o_ref.shape[1]
    total = n_q * n_chunks
    shift = n_chunks.bit_length() - 1
    col_tiles = [slice(c, c + COL_TILE) for c in range(0, tq, COL_TILE)]

    def scores(g, slot, cols):
        g = jnp.minimum(g, total - 1)
        start = pl.multiple_of((g & (n_chunks - 1)) * tk, tk)
        s = jnp.dot(k_ref[pl.ds(start, tk), :], qt_ref[g >> shift, :, cols],
                    preferred_element_type=F32)
        s_ref[slot, :, cols] = s
        mx_ref[slot, :, cols] = jnp.max(s.reshape(tk // 8, 8, COL_TILE), axis=0)

    def softmax(g, slot, p_slot, cols):
        first_chunk = (g & (n_chunks - 1)) == 0
        m_old = jnp.where(first_chunk, -jnp.inf, m_ref[:, cols])
        m_new = jnp.maximum(m_old, jnp.max(mx_ref[slot, :, cols], axis=0, keepdims=True))
        m_ref[:, cols] = m_new
        a_ref[p_slot, :, cols] = jnp.exp2(m_old - m_new)
        for r in range(0, tk, EXP_ROWS):
            p = jnp.exp2(s_ref[slot, r:r + EXP_ROWS, cols] - m_new)
            p_ref[p_slot, r:r + EXP_ROWS, cols] = p.astype(BF16)

    def accumulate(g, slot, cols):
        g = jnp.maximum(g, 0)
        par = (g >> shift) & 1
        pv = jnp.dot(vt_ref[g & (n_chunks - 1)], p_ref[slot, :, cols],
                     preferred_element_type=F32)
        acc_ref[par, :, cols] = acc_ref[par, :, cols] * a_ref[slot, :, cols] + pv

    def finalize(qb):
        par = qb & 1
        rows = pl.ds(pl.multiple_of(qb * tq, tq), tq)
        out = acc_ref[par, :dv, :] / acc_ref[par, dv:dv + 1, :]
        o_ref[rows, :] = out.T.astype(o_ref.dtype)

    p_slots = p_ref.shape[0]
    assert p_slots == 2 * PV_LAG and UNROLL % p_slots == 0

    def half_step(g, u):
        for cols in col_tiles:
            softmax(g, u % 2, u % p_slots, cols)
            scores(g + 1, (u + 1) % 2, cols)
            accumulate(g - PV_LAG, (u - PV_LAG) % p_slots, cols)

    acc_ref[...] = jnp.zeros(acc_ref.shape, F32)
    p_ref[PV_LAG:] = jnp.zeros((PV_LAG,) + p_ref.shape[1:], BF16)
    a_ref[PV_LAG:] = jnp.ones((PV_LAG,) + a_ref.shape[1:], F32)
    for cols in col_tiles:
        scores(0, 0, cols)

    def step(t, carry):
        g = UNROLL * t
        for u in range(UNROLL):
            half_step(g + u, u)

        @pl.when(jnp.logical_and(g > 0, (g & (n_chunks - 1)) == 0))
        def _():
            finalize((g >> shift) - 1)

        return carry

    lax.fori_loop(0, total // UNROLL, step, 0)
    for cols in col_tiles:
        for d in range(PV_LAG, 0, -1):
            accumulate(total - d, (total - d) % p_slots, cols)
    finalize(n_q - 1)


def _attention(qt, k, vt):
    batch, n_heads, n_q, dk, tq = qt.shape
    _, n_kv_heads, n_chunks, dv_ext, tk = vt.shape
    seq = n_q * tq
    dv = dv_ext - SUM_ROWS
    group = n_heads // n_kv_heads
    assert n_chunks & (n_chunks - 1) == 0 and n_chunks % UNROLL == 0
    return pl.pallas_call(
        functools.partial(_attn_kernel, tk=tk),
        grid=(batch, n_heads),
        in_specs=[
            pl.BlockSpec((None, None, n_q, dk, tq), lambda b, h: (b, h, 0, 0, 0)),
            pl.BlockSpec((None, None, seq, dk), lambda b, h: (b, h // group, 0, 0)),
            pl.BlockSpec((None, None, n_chunks, dv_ext, tk), lambda b, h: (b, h // group, 0, 0, 0)),
        ],
        out_specs=pl.BlockSpec((None, None, seq, dv), lambda b, h: (b, h, 0, 0)),
        out_shape=jax.ShapeDtypeStruct((batch, n_heads, seq, dv), BF16),
        scratch_shapes=[
            pltpu.VMEM((2, tk, tq), F32),
            pltpu.VMEM((2, 8, tq), F32),
            pltpu.VMEM((2 * PV_LAG, tk, tq), BF16),
            pltpu.VMEM((2 * PV_LAG, 1, tq), F32),
            pltpu.VMEM((1, tq), F32),
            pltpu.VMEM((2, dv_ext, tq), F32),
        ],
        compiler_params=_params(2),
        name="attention",
    )(qt, k, vt)


def _post_kernel(x_ref, mix_ref, qm_ref, km_ref, vm_ref, w_ref, o_ref):
    mix = jnp.concatenate([mix_ref[h] for h in range(mix_ref.shape[0])], axis=-1)
    n_mix = mix.shape[-1]
    y = x_ref[...] + jnp.dot(mix, w_ref[:n_mix, :], preferred_element_type=F32)
    o_mem = []
    for i in range(MEM_HEADS):
        lo = i * HEAD
        s = _nt_dot(qm_ref[:, lo:lo + HEAD], km_ref[:, lo:lo + HEAD])
        e = jnp.exp2(s - jnp.max(s, axis=-1, keepdims=True))
        pv = jnp.dot(e.astype(BF16), vm_ref[:, lo:lo + HEAD], preferred_element_type=F32)
        o_mem.append((pv / jnp.sum(e, axis=-1, keepdims=True)).astype(BF16))
    o_mem = jnp.concatenate(o_mem, axis=-1)
    o_ref[...] = y + jnp.dot(o_mem, w_ref[n_mix:, :], preferred_element_type=F32)


def _post(x2, mix, qm2, k_mem, v_mem, w_out, seq, tm):
    n_rows = x2.shape[0]
    per_batch = seq // tm
    n_mem = k_mem.shape[1]
    _, n_heads, _, dv = mix.shape
    row = lambda w: pl.BlockSpec((tm, w), lambda i: (i, 0))
    memspec = pl.BlockSpec((None, n_mem, MEM_W), lambda i: (i // per_batch, 0, 0))
    mixspec = pl.BlockSpec((None, n_heads, tm, dv),
                           lambda i: (i // per_batch, 0, i % per_batch, 0))
    return pl.pallas_call(
        _post_kernel,
        grid=(n_rows // tm,),
        in_specs=[row(D_MODEL), mixspec, row(MEM_W), memspec, memspec, _full(w_out.shape)],
        out_specs=row(D_MODEL),
        out_shape=jax.ShapeDtypeStruct((n_rows, D_MODEL), F32),
        compiler_params=_params(1),
        name="mix_out",
    )(x2, mix, qm2, k_mem, v_mem, w_out)


def _ffn_kernel(x_ref, g_ref, w_gu_ref, w_down_ref, o_ref):
    x = x_ref[...]
    h = (_rms(x) * g_ref[...]).astype(BF16)
    y = x
    for c in range(D_FF // FF_CHUNK):
        lo = c * FF_CHUNK
        gate = jnp.dot(h, w_gu_ref[:, lo:lo + FF_CHUNK], preferred_element_type=F32)
        up = jnp.dot(h, w_gu_ref[:, D_FF + lo:D_FF + lo + FF_CHUNK], preferred_element_type=F32)
        act = (gate * jax.nn.sigmoid(gate) * up).astype(BF16)
        y = y + jnp.dot(act, w_down_ref[lo:lo + FF_CHUNK, :], preferred_element_type=F32)
    o_ref[...] = y


def _ffn(x2, g, w_gu, w_down, tm):
    n_rows = x2.shape[0]
    row = pl.BlockSpec((tm, D_MODEL), lambda i: (i, 0))
    return pl.pallas_call(
        _ffn_kernel,
        grid=(n_rows // tm,),
        in_specs=[row, _full((1, D_MODEL)), _full(w_gu.shape), _full(w_down.shape)],
        out_specs=row,
        out_shape=jax.ShapeDtypeStruct((n_rows, D_MODEL), F32),
        compiler_params=_params(1),
        name="ffn",
    )(x2, g.reshape(1, D_MODEL), w_gu, w_down)


def _rope_tables(seq, dim):
    rows = seq // GRID_W
    row = jnp.repeat(jnp.arange(rows, dtype=F32), GRID_W)
    col = jnp.tile(jnp.arange(GRID_W, dtype=F32), rows)
    axis_dim = dim // 2
    inv = ROPE_THETA ** (-jnp.arange(0, axis_dim, 2, dtype=F32) / axis_dim)
    ang_r = row[:, None] * inv
    ang_c = col[:, None] * inv
    cos = jnp.concatenate([jnp.cos(ang_r)] * 2 + [jnp.cos(ang_c)] * 2, axis=-1)
    sin = jnp.concatenate([-jnp.sin(ang_r), jnp.sin(ang_r), -jnp.sin(ang_c), jnp.sin(ang_c)], axis=-1)
    pad = ((0, 0), (0, HEAD - dim))
    return jnp.pad(cos, pad), jnp.pad(sin, pad)


def _pad_lanes(v, width=HEAD):
    return jnp.pad(v, (0, width - v.shape[0])).reshape(1, width)


def _expand_rows(v, n_cols):
    return jnp.broadcast_to(jnp.pad(v, (0, HEAD - v.shape[0]))[:, None], (HEAD, n_cols))


def kernel(x, mem, mem_norm, norm_mix, norm_ffn, w_out, w_mem_kv, memq_norm, memk_norm, w_gate_up, w_down, mla_w_in, mla_q_a_norm, mla_w_q_b, mla_kv_a_norm, mla_w_kv_b, mla_q_norm, mla_k_norm, gqa_w_in, gqa_q_norm, gqa_k_norm):
    batch, seq, _ = x.shape
    depth = norm_mix.shape[0]
    n_rows = batch * seq
    tm_mla = _row_tile(n_rows, seq, TM_PROJ_MLA)
    tm_gqa = _row_tile(n_rows, seq, TM_PROJ_GQA)
    tk = _row_tile(n_rows, seq, ATTN_TK)
    tq = _row_tile(n_rows, seq, ATTN_TQ)
    tm_post = _row_tile(n_rows, seq, TM_DENSE)
    tm_ffn = _row_tile(n_rows, seq, TM_DENSE)
    mem_scale = HEAD ** -0.5 * LOG2E

    k_mem, v_mem = _memkv(mem, mem_norm, w_mem_kv, memk_norm)
    x2 = x.reshape(n_rows, D_MODEL)
    for i in range(depth):
        j = i // 2
        g_mix = norm_mix[i].reshape(1, D_MODEL)
        mq = memq_norm[i].reshape(1, HEAD)
        if i % 2 == 0:
            w_in = mla_w_in[j]
            o_qm = MLA_Q_RANK + MLA_KV_RANK + MLA_ROPE
            w_in = jnp.concatenate(
                [w_in[:, :o_qm], jnp.zeros((D_MODEL, HEAD - MLA_ROPE), F32), w_in[:, o_qm:]], axis=1)
            w_qb = mla_w_q_b[j].reshape(MLA_Q_RANK, MLA_HEADS, MLA_NOPE + MLA_ROPE)
            w_qb = jnp.concatenate(
                [w_qb[:, :, :MLA_NOPE].reshape(MLA_Q_RANK, -1),
                 jnp.pad(w_qb[:, :, MLA_NOPE:], ((0, 0), (0, 0), (0, HEAD - MLA_ROPE))
                         ).reshape(MLA_Q_RANK, -1)], axis=1)
            w_kvb = mla_w_kv_b[j].reshape(MLA_KV_RANK, MLA_HEADS, 2 * HEAD)
            w_kb = w_kvb[:, :, :HEAD].reshape(MLA_KV_RANK, -1)
            w_vb = w_kvb[:, :, HEAD:].reshape(MLA_KV_RANK, -1)
            consts = [g_mix, w_in.astype(BF16),
                      mla_q_a_norm[j].reshape(1, -1), w_qb.T.astype(BF16),
                      mla_kv_a_norm[j].reshape(1, -1), w_kb.astype(BF16), w_vb.T.astype(BF16),
                      _expand_rows(mla_q_norm[j][:MLA_NOPE], tm_mla),
                      _expand_rows(mla_q_norm[j][MLA_NOPE:], tm_mla),
                      _pad_lanes(mla_k_norm[j][:MLA_NOPE]), _pad_lanes(mla_k_norm[j][MLA_NOPE:]),
                      mq]
            kern = functools.partial(_proj_mla_kernel, mem_scale=mem_scale,
                                     qk_scale=(MLA_NOPE + MLA_ROPE) ** -0.5 * LOG2E)
            qt, k2, vt, qm2 = _proj_call(
                kern, x2, seq, consts, _rope_tables(seq, MLA_ROPE),
                MLA_HEADS, MLA_QK, MLA_HEADS, tm_mla, tq, tk)
        else:
            consts = [g_mix, gqa_w_in[j].astype(BF16),
                      _expand_rows(gqa_q_norm[j], tm_gqa), gqa_k_norm[j].reshape(1, HEAD), mq]
            kern = functools.partial(_proj_gqa_kernel, mem_scale=mem_scale,
                                     qk_scale=HEAD ** -0.5 * LOG2E)
            qt, k2, vt, qm2 = _proj_call(
                kern, x2, seq, consts, _rope_tables(seq, HEAD),
                GQA_HEADS, HEAD, GQA_KV_HEADS, tm_gqa, tq, tk)
        mix = _attention(qt, k2, vt)
        x2 = _post(x2, mix, qm2, k_mem[i], v_mem[i],
                   w_out[i].astype(BF16), seq, tm_post)
        x2 = _ffn(x2, norm_ffn[i], w_gate_up[i].astype(BF16), w_down[i].astype(BF16), tm_ffn)
    return x2.reshape(batch, seq, D_MODEL)
```

```python
import functools

import jax
import jax.numpy as jnp
from jax import lax
from jax.experimental import pallas as pl
from jax.experimental.pallas import tpu as pltpu

D_MODEL = 1024
GRID_W = 64
ROPE_THETA = 10000.0
EPS = 1e-6
LOG2E = 1.4426950408889634

MEM_HEADS = 4
HEAD = 128
MEM_W = MEM_HEADS * HEAD

MLA_HEADS = 8
MLA_Q_RANK = 384
MLA_KV_RANK = 256
MLA_NOPE = 128
MLA_ROPE = 64
MLA_QK = 2 * HEAD

GQA_HEADS = 8
GQA_KV_HEADS = 2

D_FF = 2816
FF_CHUNK = 256
COL_TILE = 256
EXP_ROWS = 64
SUM_ROWS = 16
UNROLL = 8
PV_LAG = 2
PROJ_SUB = 256

TM_PROJ_MLA = 512
TM_PROJ_GQA = 512
TM_DENSE = 1024
ATTN_TQ = 2048
ATTN_TK = 512

VMEM_LIMIT = 56 * 1024 * 1024

BF16 = jnp.bfloat16
F32 = jnp.float32


def _params(n_axes):
    return pltpu.CompilerParams(dimension_semantics=("arbitrary",) * n_axes,
                                vmem_limit_bytes=VMEM_LIMIT)


def _rms(x, width=None):
    n = x.shape[-1] if width is None else width
    ms = jnp.sum(x * x, axis=-1, keepdims=True) * (1.0 / n)
    return x * lax.rsqrt(ms + EPS)


def _rope(x, cos, sin, half):
    lane = lax.broadcasted_iota(jnp.int32, x.shape, 1)
    upper = pltpu.roll(x, HEAD - half, axis=1)
    lower = pltpu.roll(x, half, axis=1)
    swapped = jnp.where((lane & (2 * half - 1)) < half, upper, lower)
    return x * cos + swapped * sin


def _rms_t(xt, width=None):
    n = xt.shape[0] if width is None else width
    ms = jnp.sum(xt * xt, axis=0, keepdims=True) * (1.0 / n)
    return xt * lax.rsqrt(ms + EPS)


def _rope_t(xt, cos_t, sin_t, half):
    blocks = [xt[r:r + half] for r in range(0, xt.shape[0], half)]
    swapped = jnp.concatenate([blocks[b ^ 1] for b in range(len(blocks))], axis=0)
    return xt * cos_t + swapped * sin_t


def _nt_dot(a, b):
    return lax.dot_general(a, b, (((1,), (1,)), ((), ())), preferred_element_type=F32)


def _memkv_kernel(mem_ref, g_ref, w_ref, kn_ref, k_ref, v_ref):
    mem_n = (_rms(mem_ref[...]) * g_ref[...]).astype(BF16)
    kv = jnp.dot(mem_n, w_ref[...], preferred_element_type=F32)
    for h in range(MEM_HEADS):
        kh = _rms(kv[:, h * HEAD:(h + 1) * HEAD]) * kn_ref[...]
        k_ref[:, h * HEAD:(h + 1) * HEAD] = kh.astype(BF16)
    v_ref[...] = kv[:, MEM_W:].astype(BF16)


def _memkv(mem, mem_norm, w_mem_kv, memk_norm):
    depth = w_mem_kv.shape[0]
    batch, n_mem, _ = mem.shape
    out = jax.ShapeDtypeStruct((depth, batch, n_mem, MEM_W), BF16)
    return pl.pallas_call(
        _memkv_kernel,
        grid=(depth, batch),
        in_specs=[
            pl.BlockSpec((None, n_mem, D_MODEL), lambda l, b: (b, 0, 0)),
            pl.BlockSpec((1, D_MODEL), lambda l, b: (0, 0)),
            pl.BlockSpec((None, D_MODEL, 2 * MEM_W), lambda l, b: (l, 0, 0)),
            pl.BlockSpec((None, 1, HEAD), lambda l, b: (l, 0, 0)),
        ],
        out_specs=[
            pl.BlockSpec((None, None, n_mem, MEM_W), lambda l, b: (l, b, 0, 0)),
            pl.BlockSpec((None, None, n_mem, MEM_W), lambda l, b: (l, b, 0, 0)),
        ],
        out_shape=[out, out],
        compiler_params=_params(2),
        name="mem_kv",
    )(mem, mem_norm.reshape(1, D_MODEL), w_mem_kv.astype(BF16),
      memk_norm.reshape(depth, 1, HEAD))


def _store_vt(vt_ref, head, v_t, toks=slice(None)):
    n = v_t.shape[1]
    vt_ref[head, :HEAD, toks] = v_t.astype(BF16)
    row = lax.broadcasted_iota(jnp.int32, (SUM_ROWS, n), 0)
    vt_ref[head, HEAD:, toks] = jnp.where(row == 0, 1.0, 0.0).astype(BF16)


def _proj_mla_kernel(x_ref, g_ref, w_in_ref, qa_ref, w_qbt_ref, kva_ref, w_kb_ref, w_vbt_ref,
                     qn_ref, qp_ref, kn_ref, kp_ref, mq_ref, cos_ref, sin_ref, cos_t_ref, sin_t_ref,
                     qt_ref, k_ref, vt_ref, qm_ref, *, qk_scale, mem_scale):
    o_kv = MLA_Q_RANK
    o_pe = o_kv + MLA_KV_RANK
    o_qm = o_pe + HEAD
    half = MLA_ROPE // 4
    n_w = MLA_HEADS * HEAD
    h = (_rms(x_ref[...]) * g_ref[...]).astype(BF16)
    proj = jnp.dot(h, w_in_ref[...], preferred_element_type=F32)
    c_q = _rms(proj[:, :o_kv]) * qa_ref[...]
    c_kv = _rms(proj[:, o_kv:o_pe]) * kva_ref[...]
    q_t = jnp.dot(w_qbt_ref[...], c_q.T.astype(BF16),
                  preferred_element_type=F32)
    k = jnp.dot(c_kv.astype(BF16), w_kb_ref[...], preferred_element_type=F32)
    v_t = jnp.dot(w_vbt_ref[...], c_kv.T.astype(BF16), preferred_element_type=F32)
    k_pe = _rope(_rms(proj[:, o_pe:o_qm], MLA_ROPE) * kp_ref[...],
                 cos_ref[...], sin_ref[...], half).astype(BF16)
    cos_t = cos_t_ref[...]
    sin_t = sin_t_ref[...]
    for i in range(MLA_HEADS):
        lo = i * HEAD
        q_n = _rms_t(q_t[lo:lo + HEAD]) * qn_ref[...] * qk_scale
        q_p = _rope_t(_rms_t(q_t[n_w + lo:n_w + lo + HEAD], MLA_ROPE) * qp_ref[...],
                      cos_t, sin_t, half)
        qt_ref[i, :HEAD, :] = q_n.astype(BF16)
        qt_ref[i, HEAD:, :] = (q_p * qk_scale).astype(BF16)
        k_n = _rms(k[:, lo:lo + HEAD]) * kn_ref[...]
        k_ref[i, :, :HEAD] = k_n.astype(BF16)
        k_ref[i, :, HEAD:] = k_pe
        _store_vt(vt_ref, i, v_t[lo:lo + HEAD])
    for i in range(MEM_HEADS):
        lo = i * HEAD
        q_m = _rms(proj[:, o_qm + lo:o_qm + lo + HEAD]) * mq_ref[...] * mem_scale
        qm_ref[:, lo:lo + HEAD] = q_m.astype(BF16)


def _proj_gqa_kernel(x_ref, g_ref, w_in_ref, qn_ref, kn_ref, mq_ref, cos_ref, sin_ref,
                     cos_t_ref, sin_t_ref, qt_ref, k_ref, vt_ref, qm_ref, *, qk_scale, mem_scale):
    half = HEAD // 4
    o_k = GQA_HEADS * HEAD
    o_v = o_k + GQA_KV_HEADS * HEAD
    o_qm = o_v + GQA_KV_HEADS * HEAD
    for r in range(0, x_ref.shape[0], PROJ_SUB):
        toks = slice(r, r + PROJ_SUB)
        h = (_rms(x_ref[toks, :]) * g_ref[...]).astype(BF16)
        proj = jnp.dot(h, w_in_ref[...], preferred_element_type=F32)
        cos_t = cos_t_ref[:, toks]
        sin_t = sin_t_ref[:, toks]
        for i in range(GQA_HEADS):
            lo = i * HEAD
            q_h = _rope_t(_rms_t(proj[:, lo:lo + HEAD].T) * qn_ref[:, toks], cos_t, sin_t, half)
            qt_ref[i, :, toks] = (q_h * qk_scale).astype(BF16)
        for i in range(GQA_KV_HEADS):
            lo = o_k + i * HEAD
            k_h = _rope(_rms(proj[:, lo:lo + HEAD]) * kn_ref[...],
                        cos_ref[toks, :], sin_ref[toks, :], half)
            k_ref[i, toks, :] = k_h.astype(BF16)
        for i in range(GQA_KV_HEADS):
            lo = o_v + i * HEAD
            _store_vt(vt_ref, i, proj[:, lo:lo + HEAD].T, toks)
        for i in range(MEM_HEADS):
            lo = o_qm + i * HEAD
            q_m = _rms(proj[:, lo:lo + HEAD]) * mq_ref[...] * mem_scale
            qm_ref[toks, i * HEAD:(i + 1) * HEAD] = q_m.astype(BF16)


def _row_tile(n_rows, seq, want):
    t = min(want, seq)
    assert seq % t == 0 and n_rows % t == 0
    return t


def _full(shape):
    return pl.BlockSpec(shape, lambda i: (0,) * len(shape), pipeline_mode=pl.Buffered(1))


def _proj_call(kernel, x2, seq, consts, tables, n_heads, dk, n_kv_heads, tm, tq, tk):
    n_rows = x2.shape[0]
    batch = n_rows // seq
    n_pos = seq // tm
    row = lambda w: pl.BlockSpec((tm, w), lambda i: (i, 0))
    tab = pl.BlockSpec((tm, HEAD), lambda i: (i % n_pos, 0))
    tab_t = pl.BlockSpec((HEAD, tm), lambda i: (0, i % n_pos))
    cos, sin = tables
    rows_out = lambda w: jax.ShapeDtypeStruct((n_rows, w), BF16)
    per_block = tq // tm
    qt_spec = pl.BlockSpec(
        (None, n_heads, None, dk, tm),
        lambda i: (i // n_pos, 0, (i % n_pos) // per_block, 0, (i % n_pos) % per_block))
    qt_out = jax.ShapeDtypeStruct((batch, n_heads, seq // tq, dk, tq), BF16)
    k_spec = pl.BlockSpec((None, n_kv_heads, tm, dk), lambda i: (i // n_pos, 0, i % n_pos, 0))
    k_out = jax.ShapeDtypeStruct((batch, n_kv_heads, seq, dk), BF16)
    per_chunk = tk // tm
    vt_spec = pl.BlockSpec(
        (None, n_kv_heads, None, HEAD + SUM_ROWS, tm),
        lambda i: (i // n_pos, 0, (i % n_pos) // per_chunk, 0, (i % n_pos) % per_chunk))
    vt_out = jax.ShapeDtypeStruct((batch, n_kv_heads, seq // tk, HEAD + SUM_ROWS, tk), BF16)
    return pl.pallas_call(
        kernel,
        grid=(n_rows // tm,),
        in_specs=[row(D_MODEL)] + [_full(c.shape) for c in consts] + [tab, tab, tab_t, tab_t],
        out_specs=[qt_spec, k_spec, vt_spec, row(MEM_W)],
        out_shape=[qt_out, k_out, vt_out, rows_out(MEM_W)],
        compiler_params=_params(1),
        name=kernel.func.__name__.strip("_"),
    )(x2, *consts, cos, sin, cos.T, sin.T)


def _attn_kernel(qt_ref, k_ref, vt_ref, o_ref, s_ref, mx_ref, p_ref, a_ref, m_ref, acc_ref, *, tk):
    n_q, _, tq = qt_ref.shape
    n_chunks = vt_ref.shape[0]
    dv = o_ref.shape[1]
    total = n_q * n_chunks
    shift = n_chunks.bit_length() - 1
    col_tiles = [slice(c, c + COL_TILE) for c in range(0, tq, COL_TILE)]

    def scores(g, slot, cols):
        g = jnp.minimum(g, total - 1)
        start = pl.multiple_of((g & (n_chunks - 1)) * tk, tk)
        s = jnp.dot(k_ref[pl.ds(start, tk), :], qt_ref[g >> shift, :, cols],
                    preferred_element_type=F32)
        s_ref[slot, :, cols] = s
        mx_ref[slot, :, cols] = jnp.max(s.reshape(tk // 8, 8, COL_TILE), axis=0)

    def softmax(g, slot, p_slot, cols):
        first_chunk = (g & (n_chunks - 1)) == 0
        m_old = jnp.where(first_chunk, -jnp.inf, m_ref[:, cols])
        m_new = jnp.maximum(m_old, jnp.max(mx_ref[slot, :, cols], axis=0, keepdims=True))
        m_ref[:, cols] = m_new
        a_ref[p_slot, :, cols] = jnp.exp2(m_old - m_new)
        for r in range(0, tk, EXP_ROWS):
            p = jnp.exp2(s_ref[slot, r:r + EXP_ROWS, cols] - m_new)
            p_ref[p_slot, r:r + EXP_ROWS, cols] = p.astype(BF16)

    def accumulate(g, slot, cols):
        g = jnp.maximum(g, 0)
        par = (g >> shift) & 1
        pv = jnp.dot(vt_ref[g & (n_chunks - 1)], p_ref[slot, :, cols],
                     preferred_element_type=F32)
        acc_ref[par, :, cols] = acc_ref[par, :, cols] * a_ref[slot, :, cols] + pv

    def finalize(qb):
        par = qb & 1
        rows = pl.ds(pl.multiple_of(qb * tq, tq), tq)
        out = acc_ref[par, :dv, :] / acc_ref[par, dv:dv + 1, :]
        o_ref[rows, :] = out.T.astype(o_ref.dtype)

    p_slots = p_ref.shape[0]
    assert p_slots == 2 * PV_LAG and UNROLL % p_slots == 0

    def half_step(g, u):
        for cols in col_tiles:
            softmax(g, u % 2, u % p_slots, cols)
            scores(g + 1, (u + 1) % 2, cols)
            accumulate(g - PV_LAG, (u - PV_LAG) % p_slots, cols)

    acc_ref[...] = jnp.zeros(acc_ref.shape, F32)
    p_ref[PV_LAG:] = jnp.zeros((PV_LAG,) + p_ref.shape[1:], BF16)
    a_ref[PV_LAG:] = jnp.ones((PV_LAG,) + a_ref.shape[1:], F32)
    for cols in col_tiles:
        scores(0, 0, cols)

    def step(t, carry):
        g = UNROLL * t
        for u in range(UNROLL):
            half_step(g + u, u)

        @pl.when(jnp.logical_and(g > 0, (g & (n_chunks - 1)) == 0))
        def _():
            finalize((g >> shift) - 1)

        return carry

    lax.fori_loop(0, total // UNROLL, step, 0)
    for cols in col_tiles:
        for d in range(PV_LAG, 0, -1):
            accumulate(total - d, (total - d) % p_slots, cols)
    finalize(n_q - 1)


def _attention(qt, k, vt):
    batch, n_heads, n_q, dk, tq = qt.shape
    _, n_kv_heads, n_chunks, dv_ext, tk = vt.shape
    seq = n_q * tq
    dv = dv_ext - SUM_ROWS
    group = n_heads // n_kv_heads
    assert n_chunks & (n_chunks - 1) == 0 and n_chunks % UNROLL == 0
    return pl.pallas_call(
        functools.partial(_attn_kernel, tk=tk),
        grid=(batch, n_heads),
        in_specs=[
            pl.BlockSpec((None, None, n_q, dk, tq), lambda b, h: (b, h, 0, 0, 0)),
            pl.BlockSpec((None, None, seq, dk), lambda b, h: (b, h // group, 0, 0)),
            pl.BlockSpec((None, None, n_chunks, dv_ext, tk), lambda b, h: (b, h // group, 0, 0, 0)),
        ],
        out_specs=pl.BlockSpec((None, None, seq, dv), lambda b, h: (b, h, 0, 0)),
        out_shape=jax.ShapeDtypeStruct((batch, n_heads, seq, dv), BF16),
        scratch_shapes=[
            pltpu.VMEM((2, tk, tq), F32),
            pltpu.VMEM((2, 8, tq), F32),
            pltpu.VMEM((2 * PV_LAG, tk, tq), BF16),
            pltpu.VMEM((2 * PV_LAG, 1, tq), F32),
            pltpu.VMEM((1, tq), F32),
            pltpu.VMEM((2, dv_ext, tq), F32),
        ],
        compiler_params=_params(2),
        name="attention",
    )(qt, k, vt)


def _post_kernel(x_ref, mix_ref, qm_ref, km_ref, vm_ref, w_ref, o_ref):
    mix = jnp.concatenate([mix_ref[h] for h in range(mix_ref.shape[0])], axis=-1)
    n_mix = mix.shape[-1]
    y = x_ref[...] + jnp.dot(mix, w_ref[:n_mix, :], preferred_element_type=F32)
    o_mem = []
    for i in range(MEM_HEADS):
        lo = i * HEAD
        s = _nt_dot(qm_ref[:, lo:lo + HEAD], km_ref[:, lo:lo + HEAD])
        e = jnp.exp2(s - jnp.max(s, axis=-1, keepdims=True))
        pv = jnp.dot(e.astype(BF16), vm_ref[:, lo:lo + HEAD], preferred_element_type=F32)
        o_mem.append((pv / jnp.sum(e, axis=-1, keepdims=True)).astype(BF16))
    o_mem = jnp.concatenate(o_mem, axis=-1)
    o_ref[...] = y + jnp.dot(o_mem, w_ref[n_mix:, :], preferred_element_type=F32)


def _post(x2, mix, qm2, k_mem, v_mem, w_out, seq, tm):
    n_rows = x2.shape[0]
    per_batch = seq // tm
    n_mem = k_mem.shape[1]
    _, n_heads, _, dv = mix.shape
    row = lambda w: pl.BlockSpec((tm, w), lambda i: (i, 0))
    memspec = pl.BlockSpec((None, n_mem, MEM_W), lambda i: (i // per_batch, 0, 0))
    mixspec = pl.BlockSpec((None, n_heads, tm, dv),
                           lambda i: (i // per_batch, 0, i % per_batch, 0))
    return pl.pallas_call(
        _post_kernel,
        grid=(n_rows // tm,),
        in_specs=[row(D_MODEL), mixspec, row(MEM_W), memspec, memspec, _full(w_out.shape)],
        out_specs=row(D_MODEL),
        out_shape=jax.ShapeDtypeStruct((n_rows, D_MODEL), F32),
        compiler_params=_params(1),
        name="mix_out",
    )(x2, mix, qm2, k_mem, v_mem, w_out)


def _ffn_kernel(x_ref, g_ref, w_gu_ref, w_down_ref, o_ref):
    x = x_ref[...]
    h = (_rms(x) * g_ref[...]).astype(BF16)
    y = x
    for c in range(D_FF // FF_CHUNK):
        lo = c * FF_CHUNK
        gate = jnp.dot(h, w_gu_ref[:, lo:lo + FF_CHUNK], preferred_element_type=F32)
        up = jnp.dot(h, w_gu_ref[:, D_FF + lo:D_FF + lo + FF_CHUNK], preferred_element_type=F32)
        act = (gate * jax.nn.sigmoid(gate) * up).astype(BF16)
        y = y + jnp.dot(act, w_down_ref[lo:lo + FF_CHUNK, :], preferred_element_type=F32)
    o_ref[...] = y


def _ffn(x2, g, w_gu, w_down, tm):
    n_rows = x2.shape[0]
    row = pl.BlockSpec((tm, D_MODEL), lambda i: (i, 0))
    return pl.pallas_call(
        _ffn_kernel,
        grid=(n_rows // tm,),
        in_specs=[row, _full((1, D_MODEL)), _full(w_gu.shape), _full(w_down.shape)],
        out_specs=row,
        out_shape=jax.ShapeDtypeStruct((n_rows, D_MODEL), F32),
        compiler_params=_params(1),
        name="ffn",
    )(x2, g.reshape(1, D_MODEL), w_gu, w_down)


def _rope_tables(seq, dim):
    rows = seq // GRID_W
    row = jnp.repeat(jnp.arange(rows, dtype=F32), GRID_W)
    col = jnp.tile(jnp.arange(GRID_W, dtype=F32), rows)
    axis_dim = dim // 2
    inv = ROPE_THETA ** (-jnp.arange(0, axis_dim, 2, dtype=F32) / axis_dim)
    ang_r = row[:, None] * inv
    ang_c = col[:, None] * inv
    cos = jnp.concatenate([jnp.cos(ang_r)] * 2 + [jnp.cos(ang_c)] * 2, axis=-1)
    sin = jnp.concatenate([-jnp.sin(ang_r), jnp.sin(ang_r), -jnp.sin(ang_c), jnp.sin(ang_c)], axis=-1)
    pad = ((0, 0), (0, HEAD - dim))
    return jnp.pad(cos, pad), jnp.pad(sin, pad)


def _pad_lanes(v, width=HEAD):
    return jnp.pad(v, (0, width - v.shape[0])).reshape(1, width)


def _expand_rows(v, n_cols):
    return jnp.broadcast_to(jnp.pad(v, (0, HEAD - v.shape[0]))[:, None], (HEAD, n_cols))


def kernel(x, mem, mem_norm, norm_mix, norm_ffn, w_out, w_mem_kv, memq_norm, memk_norm, w_gate_up, w_down, mla_w_in, mla_q_a_norm, mla_w_q_b, mla_kv_a_norm, mla_w_kv_b, mla_q_norm, mla_k_norm, gqa_w_in, gqa_q_norm, gqa_k_norm):
    batch, seq, _ = x.shape
    depth = norm_mix.shape[0]
    n_rows = batch * seq
    tm_mla = _row_tile(n_rows, seq, TM_PROJ_MLA)
    tm_gqa = _row_tile(n_rows, seq, TM_PROJ_GQA)
    tk = _row_tile(n_rows, seq, ATTN_TK)
    tq = _row_tile(n_rows, seq, ATTN_TQ)
    tm_post = _row_tile(n_rows, seq, TM_DENSE)
    tm_ffn = _row_tile(n_rows, seq, TM_DENSE)
    mem_scale = HEAD ** -0.5 * LOG2E

    k_mem, v_mem = _memkv(mem, mem_norm, w_mem_kv, memk_norm)
    x2 = x.reshape(n_rows, D_MODEL)
    for i in range(depth):
        j = i // 2
        g_mix = norm_mix[i].reshape(1, D_MODEL)
        mq = memq_norm[i].reshape(1, HEAD)
        if i % 2 == 0:
            w_in = mla_w_in[j]
            o_qm = MLA_Q_RANK + MLA_KV_RANK + MLA_ROPE
            w_in = jnp.concatenate(
                [w_in[:, :o_qm], jnp.zeros((D_MODEL, HEAD - MLA_ROPE), F32), w_in[:, o_qm:]], axis=1)
            w_qb = mla_w_q_b[j].reshape(MLA_Q_RANK, MLA_HEADS, MLA_NOPE + MLA_ROPE)
            w_qb = jnp.concatenate(
                [w_qb[:, :, :MLA_NOPE].reshape(MLA_Q_RANK, -1),
                 jnp.pad(w_qb[:, :, MLA_NOPE:], ((0, 0), (0, 0), (0, HEAD - MLA_ROPE))
                         ).reshape(MLA_Q_RANK, -1)], axis=1)
            w_kvb = mla_w_kv_b[j].reshape(MLA_KV_RANK, MLA_HEADS, 2 * HEAD)
            w_kb = w_kvb[:, :, :HEAD].reshape(MLA_KV_RANK, -1)
            w_vb = w_kvb[:, :, HEAD:].reshape(MLA_KV_RANK, -1)
            consts = [g_mix, w_in.astype(BF16),
                      mla_q_a_norm[j].reshape(1, -1), w_qb.T.astype(BF16),
                      mla_kv_a_norm[j].reshape(1, -1), w_kb.astype(BF16), w_vb.T.astype(BF16),
                      _expand_rows(mla_q_norm[j][:MLA_NOPE], tm_mla),
                      _expand_rows(mla_q_norm[j][MLA_NOPE:], tm_mla),
                      _pad_lanes(mla_k_norm[j][:MLA_NOPE]), _pad_lanes(mla_k_norm[j][MLA_NOPE:]),
                      mq]
            kern = functools.partial(_proj_mla_kernel, mem_scale=mem_scale,
                                     qk_scale=(MLA_NOPE + MLA_ROPE) ** -0.5 * LOG2E)
            qt, k2, vt, qm2 = _proj_call(
                kern, x2, seq, consts, _rope_tables(seq, MLA_ROPE),
                MLA_HEADS, MLA_QK, MLA_HEADS, tm_mla, tq, tk)
        else:
            consts = [g_mix, gqa_w_in[j].astype(BF16),
                      _expand_rows(gqa_q_norm[j], tm_gqa), gqa_k_norm[j].reshape(1, HEAD), mq]
            kern = functools.partial(_proj_gqa_kernel, mem_scale=mem_scale,
                                     qk_scale=HEAD ** -0.5 * LOG2E)
            qt, k2, vt, qm2 = _proj_call(
                kern, x2, seq, consts, _rope_tables(seq, HEAD),
                GQA_HEADS, HEAD, GQA_KV_HEADS, tm_gqa, tq, tk)
        mix = _attention(qt, k2, vt)
        x2 = _post(x2, mix, qm2, k_mem[i], v_mem[i],
                   w_out[i].astype(BF16), seq, tm_post)
        x2 = _ffn(x2, norm_ffn[i], w_gate_up[i].astype(BF16), w_down[i].astype(BF16), tm_ffn)
    return x2.reshape(batch, seq, D_MODEL)
```
